```python
import jax, jax.numpy as jnp
from jax import lax
import numpy as np

D_MODEL = 1024
BATCH = 4
SEQ = 8192
DEPTH = 2

N_EVEN = (DEPTH + 1) // 2
N_ODD = DEPTH // 2
RMS_EPS = 1e-6
D_CONV = D_MODEL // 2
CONV_WIDTH = 3
D_POOL = D_MODEL // 2
POOL_WINDOWS = (2, 4, 8, 16)
POOL_GROUPS = len(POOL_WINDOWS)
POOL_GC = D_POOL // POOL_GROUPS
MAX_WIN = max(POOL_WINDOWS)
MIX_IN = 3 * D_CONV + D_POOL
MIX_OUT = D_CONV + D_POOL
HEAD_SIZE = 64
RWKV_HEADS = D_MODEL // HEAD_SIZE
DECAY_LORA = 64
AAA_LORA = 64
GATE_LORA = 160
LNX_EPS = 1e-5 * HEAD_SIZE
D_FF = 2816
N_EXPERTS = 8
TOP_K = 2

kernel_name = 'hybrid_conv_pool_rwkv7_moe'


def rmsnorm(x, g):
    xf = x.astype(jnp.float32)
    y = xf * lax.rsqrt(jnp.mean(xf * xf, axis=-1, keepdims=True) + RMS_EPS)
    return (y * g.astype(jnp.float32)).astype(x.dtype)


def short_conv(u, w):
    S = u.shape[1]
    up = jnp.pad(u, ((0, 0), (CONV_WIDTH - 1, 0), (0, 0)))
    return up[:, 0:S] * w[0] + up[:, 1:S + 1] * w[1] + up[:, 2:S + 2] * w[2]


def multiscale_pool(u, w_pool, scale):
    S = u.shape[1]
    uf = u.astype(jnp.float32)
    cs = jnp.pad(jnp.cumsum(uf, axis=1), ((0, 0), (MAX_WIN, 0), (0, 0)))
    pos = jnp.arange(S)
    outs = []
    for gi, win in enumerate(POOL_WINDOWS):
        sl = slice(gi * POOL_GC, (gi + 1) * POOL_GC)
        total = cs[:, MAX_WIN:MAX_WIN + S, sl] - cs[:, MAX_WIN - win:MAX_WIN - win + S, sl]
        cnt = jnp.minimum(pos + 1, win).astype(jnp.float32)[None, :, None]
        outs.append(total / cnt - uf[:, :, sl])
    p = jnp.stack(outs, axis=2)
    y = jnp.einsum('bsgc,gcd->bsgd', p, w_pool.astype(jnp.float32)).reshape(uf.shape)
    return (y * scale.astype(jnp.float32)).astype(u.dtype)


def conv_pool_mix(h, w_in, conv_w, pool_w, pool_scale, w_out):
    z = h @ w_in
    b_gate, c_gate, v_conv, v_pool = jnp.split(z, [D_CONV, 2 * D_CONV, 3 * D_CONV], axis=-1)
    y_conv = b_gate * short_conv(c_gate * v_conv, conv_w)
    y_pool = multiscale_pool(v_pool, pool_w, pool_scale)
    return jnp.concatenate([y_conv, y_pool], axis=-1) @ w_out


def rwkv7_mix(h, mu, w_r, w_k, w_v, w_o, w0, w1, w2, a0, a1, a2, g1, g2, k_k, k_a, r_k, ln_g, ln_b):
    B, S, D = h.shape
    f32 = jnp.float32
    h_prev = jnp.pad(h, ((0, 0), (1, 0), (0, 0)))[:, :S]
    xx = h_prev - h
    xr, xw, xk, xv, xa, xg = [h + xx * mu[i] for i in range(6)]
    r = xr @ w_r
    k = xk @ w_k
    v = xv @ w_v
    w_log = -jax.nn.softplus(-(w0 + jnp.tanh(xw @ w1) @ w2).astype(f32)) - 0.5
    decay = jnp.exp(-jnp.exp(w_log))
    a = jax.nn.sigmoid((a0 + (xa @ a1) @ a2).astype(f32))
    g = jax.nn.sigmoid(xg @ g1) @ g2

    def heads(t):
        return t.astype(f32).reshape(B, S, RWKV_HEADS, HEAD_SIZE)

    kk = heads(k * k_k)
    kk = kk / jnp.maximum(jnp.sqrt(jnp.sum(kk * kk, axis=-1, keepdims=True)), 1e-12)
    k = k.astype(f32) * (1.0 + (a - 1.0) * k_a.astype(f32))
    rh, wh, kh, vh, ah = heads(r), heads(decay), heads(k), heads(v), heads(a)

    def to_seq(t):
        return jnp.moveaxis(t, 1, 0)

    def step(state, inp):
        r_t, w_t, k_t, v_t, kk_t, a_t = inp
        sa = jnp.einsum('bhij,bhj->bhi', state, -kk_t)
        state = (state * w_t[:, :, None, :] + sa[..., None] * (kk_t * a_t)[:, :, None, :]
                 + v_t[..., None] * k_t[:, :, None, :])
        y_t = jnp.einsum('bhij,bhj->bhi', state, r_t)
        return state, y_t

    state0 = jnp.zeros((B, RWKV_HEADS, HEAD_SIZE, HEAD_SIZE), f32)
    _, ys = lax.scan(step, state0, (to_seq(rh), to_seq(wh), to_seq(kh), to_seq(vh), to_seq(kk), to_seq(ah)))
    y = jnp.moveaxis(ys, 0, 1)
    mean = jnp.mean(y, axis=-1, keepdims=True)
    var = jnp.mean(jnp.square(y - mean), axis=-1, keepdims=True)
    yn = ((y - mean) * lax.rsqrt(var + LNX_EPS)).reshape(B, S, D) * ln_g.astype(f32) + ln_b.astype(f32)
    bonus = (jnp.sum(rh * kh * r_k.astype(f32), axis=-1, keepdims=True) * vh).reshape(B, S, D)
    out = ((yn + bonus) * g.astype(f32)).astype(h.dtype) @ w_o
    return out


def swiglu(t, wg, wu, wd):
    return (jax.nn.silu(t @ wg) * (t @ wu)) @ wd


def moe_swiglu(h, router, wg, wu, wd):
    B, S, D = h.shape
    t = h.reshape(B * S, D)
    logits = (t @ router).astype(jnp.float32)
    top_val, top_idx = lax.top_k(logits, TOP_K)
    top_w = jax.nn.softmax(top_val, axis=-1)
    combine = jnp.sum(top_w[..., None] * jax.nn.one_hot(top_idx, N_EXPERTS, dtype=jnp.float32), axis=1)
    out = jnp.zeros((B * S, D), jnp.float32)
    for e in range(N_EXPERTS):
        out = out + combine[:, e:e + 1] * swiglu(t, wg[e], wu[e], wd[e]).astype(jnp.float32)
    return out.reshape(B, S, D).astype(h.dtype)


def setup_inputs(seed: int = 0) -> dict:
    key = jax.random.key(seed)
    ks = iter(jax.random.split(key, 40))

    def nrm(shape, scale):
        return jax.random.normal(next(ks), shape, jnp.float32) * scale

    def gain(shape):
        return 1.0 + nrm(shape, 0.05)

    D, F, E = D_MODEL, D_FF, N_EXPERTS
    return {
        'x': nrm((BATCH, SEQ, D), 1.0),
        'c': nrm((BATCH, D), 1.0),
        'ada_w': nrm((DEPTH, D, 6 * D), 0.02),
        'ada_b': nrm((DEPTH, 6 * D), 0.02),
        'norm_g': gain((DEPTH, 4, D)),
        'mix_w_in': nrm((N_EVEN, D, MIX_IN), D ** -0.5),
        'conv_w': nrm((N_EVEN, CONV_WIDTH, D_CONV), CONV_WIDTH ** -0.5),
        'pool_w': nrm((N_EVEN, POOL_GROUPS, POOL_GC, POOL_GC), POOL_GC ** -0.5),
        'pool_scale': gain((N_EVEN, D_POOL)),
        'mix_w_out': nrm((N_EVEN, MIX_OUT, D), MIX_OUT ** -0.5),
        'ffn_w_gate': nrm((N_EVEN, D, F), D ** -0.5),
        'ffn_w_up': nrm((N_EVEN, D, F), D ** -0.5),
        'ffn_w_down': nrm((N_EVEN, F, D), F ** -0.5),
        'rwkv_mu': jax.random.uniform(next(ks), (N_ODD, 6, D), jnp.float32),
        'rwkv_w_r': nrm((N_ODD, D, D), D ** -0.5),
        'rwkv_w_k': nrm((N_ODD, D, D), D ** -0.5),
        'rwkv_w_v': nrm((N_ODD, D, D), D ** -0.5),
        'rwkv_w_o': nrm((N_ODD, D, D), D ** -0.5),
        'rwkv_w0': nrm((N_ODD, D), 0.5),
        'rwkv_w1': nrm((N_ODD, D, DECAY_LORA), D ** -0.5),
        'rwkv_w2': nrm((N_ODD, DECAY_LORA, D), DECAY_LORA ** -0.5),
        'rwkv_a0': nrm((N_ODD, D), 0.1),
        'rwkv_a1': nrm((N_ODD, D, AAA_LORA), D ** -0.5),
        'rwkv_a2': nrm((N_ODD, AAA_LORA, D), AAA_LORA ** -0.5),
        'rwkv_g1': nrm((N_ODD, D, GATE_LORA), D ** -0.5),
        'rwkv_g2': nrm((N_ODD, GATE_LORA, D), GATE_LORA ** -0.5),
        'rwkv_k_k': gain((N_ODD, D)),
        'rwkv_k_a': gain((N_ODD, D)),
        'rwkv_r_k': nrm((N_ODD, RWKV_HEADS, HEAD_SIZE), 0.1),
        'rwkv_ln_g': gain((N_ODD, D)),
        'rwkv_ln_b': nrm((N_ODD, D), 0.02),
        'moe_router': nrm((N_ODD, D, E), D ** -0.5),
        'moe_w_gate': nrm((N_ODD, E, D, F), D ** -0.5),
        'moe_w_up': nrm((N_ODD, E, D, F), D ** -0.5),
        'moe_w_down': nrm((N_ODD, E, F, D), F ** -0.5),
    }


def reference(x, c, ada_w, ada_b, norm_g, mix_w_in, conv_w, pool_w, pool_scale, mix_w_out,
              ffn_w_gate, ffn_w_up, ffn_w_down, rwkv_mu, rwkv_w_r, rwkv_w_k, rwkv_w_v, rwkv_w_o,
              rwkv_w0, rwkv_w1, rwkv_w2, rwkv_a0, rwkv_a1, rwkv_a2, rwkv_g1, rwkv_g2,
              rwkv_k_k, rwkv_k_a, rwkv_r_k, rwkv_ln_g, rwkv_ln_b,
              moe_router, moe_w_gate, moe_w_up, moe_w_down):
    cond = jax.nn.silu(c)
    for layer in range(DEPTH):
        mod = cond @ ada_w[layer] + ada_b[layer]
        sh_m, sc_m, gt_m, sh_f, sc_f, gt_f = [m[:, None, :] for m in jnp.split(mod, 6, axis=-1)]
        g_pre_m, g_post_m, g_pre_f, g_post_f = norm_g[layer]
        i = layer // 2
        h = rmsnorm(x, g_pre_m) * (1 + sc_m) + sh_m
        if layer % 2 == 0:
            y = conv_pool_mix(h, mix_w_in[i], conv_w[i], pool_w[i], pool_scale[i], mix_w_out[i])
        else:
            y = rwkv7_mix(h, rwkv_mu[i], rwkv_w_r[i], rwkv_w_k[i], rwkv_w_v[i], rwkv_w_o[i],
                          rwkv_w0[i], rwkv_w1[i], rwkv_w2[i], rwkv_a0[i], rwkv_a1[i], rwkv_a2[i],
                          rwkv_g1[i], rwkv_g2[i], rwkv_k_k[i], rwkv_k_a[i], rwkv_r_k[i],
                          rwkv_ln_g[i], rwkv_ln_b[i])
        x = x + gt_m * rmsnorm(y, g_post_m)
        h = rmsnorm(x, g_pre_f) * (1 + sc_f) + sh_f
        if layer % 2 == 0:
            y = swiglu(h, ffn_w_gate[i], ffn_w_up[i], ffn_w_down[i])
        else:
            y = moe_swiglu(h, moe_router[i], moe_w_gate[i], moe_w_up[i], moe_w_down[i])
        x = x + gt_f * rmsnorm(y, g_post_f)
    return x
```

```python
import functools

import jax
import jax.numpy as jnp
from jax import lax
from jax.experimental import pallas as pl
from jax.experimental.pallas import tpu as pltpu

F32 = jnp.float32
BF16 = jnp.bfloat16

RMS_EPS = 1e-6
HEAD = 64
QUAD = 4 * HEAD
CHUNK = 64
POOL_WINDOWS = (2, 4, 8, 16)
MAX_WIN = 16
CONV_HALO = 8
LANE = 128
TOP_K = 2
VMEM_LIMIT = 56 * 1024 * 1024


def _dot(a, b):
    return jnp.dot(a, b, preferred_element_type=F32)


def _dot_nt(a, b):
    return lax.dot_general(a, b, (((1,), (1,)), ((), ())), preferred_element_type=F32)


def _dot_f32(a, b):
    return jnp.dot(a, b, preferred_element_type=F32, precision=lax.Precision.HIGHEST)


def _rms(x):
    return x * lax.rsqrt(jnp.mean(x * x, axis=-1, keepdims=True) + RMS_EPS)


def _params(sem):
    return pltpu.CompilerParams(dimension_semantics=sem, vmem_limit_bytes=VMEM_LIMIT)


def _const_spec(shape):
    nd = len(shape)
    return pl.BlockSpec(shape, lambda *_: (0,) * nd, pipeline_mode=pl.Buffered(1))


def _ada_kernel(c_ref, w_ref, b_ref, o_ref):
    c = c_ref[...]
    cond = c * jax.nn.sigmoid(c)
    o_ref[0] = _dot_f32(cond, w_ref[0]) + b_ref[0]


def _ada(c, ada_w, ada_b):
    depth, d, d6 = ada_w.shape
    b = c.shape[0]
    nblk = d6 // d
    out = pl.pallas_call(
        _ada_kernel,
        out_shape=jax.ShapeDtypeStruct((depth, b, d6), F32),
        grid=(depth, nblk),
        in_specs=[
            pl.BlockSpec((b, d), lambda l, n: (0, 0)),
            pl.BlockSpec((1, d, d), lambda l, n: (l, 0, n)),
            pl.BlockSpec((1, 1, d), lambda l, n: (l, 0, n)),
        ],
        out_specs=pl.BlockSpec((1, b, d), lambda l, n: (l, 0, n)),
        compiler_params=_params(("arbitrary", "arbitrary")),
        name="ada_mod",
    )(c, ada_w, ada_b.reshape(depth, 1, d6))
    return out.reshape(depth, b, nblk, d)


def _mix_kernel(x_ref, mod_ref, g_ref, w_in_ref, conv_w_ref, pool_w_ref, pool_s_ref, w_out_ref,
                o_ref, u_ext, v_ext, *, tm, dc, gc):
    j = pl.program_id(1)

    @pl.when(j == 0)
    def _():
        u_ext[0:CONV_HALO, :] = jnp.zeros((CONV_HALO, dc), F32)
        v_ext[0:MAX_WIN, :] = jnp.zeros((MAX_WIN, len(POOL_WINDOWS) * gc), F32)

    x = x_ref[0]
    sh, sc, gt = mod_ref[0, 0:1, :], mod_ref[0, 1:2, :], mod_ref[0, 2:3, :]
    h = (_rms(x) * g_ref[0:1, :]) * (1.0 + sc) + sh
    z = _dot(h.astype(BF16), w_in_ref[...])
    b_gate, c_gate, v_conv, v_pool = z[:, 0:dc], z[:, dc:2 * dc], z[:, 2 * dc:3 * dc], z[:, 3 * dc:]

    u = c_gate * v_conv
    u_ext[CONV_HALO:CONV_HALO + tm, :] = u
    um1 = u_ext[CONV_HALO - 1:CONV_HALO - 1 + tm, :]
    um2 = u_ext[CONV_HALO - 2:CONV_HALO - 2 + tm, :]
    y_conv = b_gate * (um2 * conv_w_ref[0:1, :] + um1 * conv_w_ref[1:2, :] + u * conv_w_ref[2:3, :])
    u_ext[0:CONV_HALO, :] = u[tm - CONV_HALO:tm, :]

    v_ext[MAX_WIN:MAX_WIN + tm, :] = v_pool
    pos = j * tm + lax.broadcasted_iota(jnp.int32, (tm, 1), 0)
    pieces = [y_conv.astype(BF16)]
    for gi, win in enumerate(POOL_WINDOWS):
        lo, hi = gi * gc, (gi + 1) * gc
        tot = v_pool[:, lo:hi]
        for d in range(1, win):
            tot = tot + v_ext[MAX_WIN - d:MAX_WIN - d + tm, lo:hi]
        cnt = jnp.minimum(pos + 1, win).astype(F32)
        p = tot / cnt - v_pool[:, lo:hi]
        yp = _dot(p.astype(BF16), pool_w_ref[gi]) * pool_s_ref[0:1, lo:hi]
        pieces.append(yp.astype(BF16))
    v_ext[0:MAX_WIN, :] = v_pool[tm - MAX_WIN:tm, :]

    y = _dot(jnp.concatenate(pieces, axis=-1), w_out_ref[...])
    o_ref[0] = x + gt * (_rms(y) * g_ref[1:2, :])


def _mixer(x, mod, norm_g, w_in, conv_w, pool_w, pool_scale, w_out, tm):
    b, s, d = x.shape
    dc = conv_w.shape[1]
    ng, gc = pool_w.shape[0], pool_w.shape[1]
    kern = functools.partial(_mix_kernel, tm=tm, dc=dc, gc=gc)
    return pl.pallas_call(
        kern,
        out_shape=jax.ShapeDtypeStruct((b, s, d), F32),
        grid=(b, s // tm),
        in_specs=[
            pl.BlockSpec((1, tm, d), lambda i, j: (i, j, 0)),
            pl.BlockSpec((1,) + mod.shape[1:], lambda i, j: (i, 0, 0)),
            _const_spec(norm_g.shape),
            _const_spec(w_in.shape),
            _const_spec(conv_w.shape),
            _const_spec(pool_w.shape),
            _const_spec((1, ng * gc)),
            _const_spec(w_out.shape),
        ],
        out_specs=pl.BlockSpec((1, tm, d), lambda i, j: (i, j, 0)),
        scratch_shapes=[pltpu.VMEM((CONV_HALO + tm, dc), F32), pltpu.VMEM((MAX_WIN + tm, ng * gc), F32)],
        compiler_params=_params(("arbitrary", "arbitrary")),
        name="conv_pool_mixer",
    )(x, mod, norm_g, w_in.astype(BF16), conv_w, pool_w.astype(BF16), pool_scale.reshape(1, -1),
      w_out.astype(BF16))


def _ffn_kernel(x_ref, mod_ref, g_ref, wg_ref, wu_ref, wd_ref, o_ref):
    x = x_ref[0]
    sh, sc, gt = mod_ref[0, 3:4, :], mod_ref[0, 4:5, :], mod_ref[0, 5:6, :]
    h = ((_rms(x) * g_ref[2:3, :]) * (1.0 + sc) + sh).astype(BF16)
    g = _dot(h, wg_ref[...])
    u = _dot(h, wu_ref[...])
    a = (g * jax.nn.sigmoid(g) * u).astype(BF16)
    y = _dot(a, wd_ref[...])
    o_ref[0] = x + gt * (_rms(y) * g_ref[3:4, :])


def _ffn(x, mod, norm_g, wg, wu, wd, tm):
    b, s, d = x.shape
    return pl.pallas_call(
        _ffn_kernel,
        out_shape=jax.ShapeDtypeStruct((b, s, d), F32),
        grid=(b, s // tm),
        in_specs=[
            pl.BlockSpec((1, tm, d), lambda i, j: (i, j, 0)),
            pl.BlockSpec((1,) + mod.shape[1:], lambda i, j: (i, 0, 0)),
            _const_spec(norm_g.shape),
            _const_spec(wg.shape),
            _const_spec(wu.shape),
            _const_spec(wd.shape),
        ],
        out_specs=pl.BlockSpec((1, tm, d), lambda i, j: (i, j, 0)),
        compiler_params=_params(("arbitrary", "arbitrary")),
        name="dense_swiglu",
    )(x, mod, norm_g, wg.astype(BF16), wu.astype(BF16), wd.astype(BF16))


def _split_hi_lo(x):
    hi = x.astype(BF16)
    lo = (x - hi.astype(F32)).astype(BF16)
    return hi, lo


def _group_sum(x, ones_bd):
    hi, lo = _split_hi_lo(x)
    return _dot(hi, ones_bd) + _dot(lo, ones_bd)


def _rwkv_proj_kernel(x_ref, mod_ref, g_ref, mu_ref, wr_ref, wk_ref, wv_ref, w1_ref, w2_ref, a1_ref, a2_ref,
                      g1_ref, g2_ref, vec_ref, ones_ref,
                      r_out, k_out, v_out, kk_out, a_out, ld_out, g_out, h_ext, *, tm):
    j = pl.program_id(1)
    d = x_ref.shape[2]

    @pl.when(j == 0)
    def _():
        h_ext[0:CONV_HALO, :] = jnp.zeros((CONV_HALO, d), F32)

    x = x_ref[0]
    sh, sc = mod_ref[0, 0:1, :], mod_ref[0, 1:2, :]
    h = (_rms(x) * g_ref[0:1, :]) * (1.0 + sc) + sh
    h_ext[CONV_HALO:CONV_HALO + tm, :] = h
    h_prev = h_ext[CONV_HALO - 1:CONV_HALO - 1 + tm, :]
    h_ext[0:CONV_HALO, :] = h[tm - CONV_HALO:tm, :]
    xx = h_prev - h

    def mixed(i):
        return (h + xx * mu_ref[i:i + 1, :]).astype(BF16)

    w0, a0, k_k, k_a = vec_ref[0:1, :], vec_ref[1:2, :], vec_ref[2:3, :], vec_ref[3:4, :]
    r = _dot(mixed(0), wr_ref[...])
    w_lora = _dot(jnp.tanh(_dot(mixed(1), w1_ref[...])).astype(BF16), w2_ref[...])
    k = _dot(mixed(2), wk_ref[...])
    v = _dot(mixed(3), wv_ref[...])
    a_lora = _dot(_dot(mixed(4), a1_ref[...]).astype(BF16), a2_ref[...])
    gate = _dot(jax.nn.sigmoid(_dot(mixed(5), g1_ref[...])).astype(BF16), g2_ref[...])

    w_log = -jax.nn.softplus(-(w0 + w_lora)) - 0.5
    a = jax.nn.sigmoid(a0 + a_lora)
    kk = k * k_k
    nrm = jnp.sqrt(_group_sum(kk * kk, ones_ref[...]))
    kk = kk / jnp.maximum(nrm, 1e-12)

    r_out[0] = r
    k_out[0] = k * (1.0 + (a - 1.0) * k_a)
    v_out[0] = v
    kk_out[0] = kk
    a_out[0] = a
    ld_out[0] = -jnp.exp(w_log)
    g_out[0] = gate


def _pad_cols(w, n):
    return jnp.pad(w, ((0, 0), (0, n - w.shape[1])))


def _pad_rows(w, n):
    return jnp.pad(w, ((0, n - w.shape[0]), (0, 0)))


def _ceil_to(n, m):
    return -(-n // m) * m


def _head_ones(d):
    idx = jnp.arange(d) // HEAD
    return (idx[:, None] == idx[None, :]).astype(BF16)


def _rwkv_proj(x, mod, norm_g, mu, w_r, w_k, w_v, w0, w1, w2, a0, a1, a2, g1, g2, k_k, k_a, tm):
    b, s, d = x.shape
    lw, la, lg = (_ceil_to(w.shape[1], LANE) for w in (w1, a1, g1))
    w1p, w2p = _pad_cols(w1, lw).astype(BF16), _pad_rows(w2, lw).astype(BF16)
    a1p, a2p = _pad_cols(a1, la).astype(BF16), _pad_rows(a2, la).astype(BF16)
    g1p, g2p = _pad_cols(g1, lg).astype(BF16), _pad_rows(g2, lg).astype(BF16)
    vecs = jnp.stack([w0, a0, k_k, k_a], axis=0)
    consts = [norm_g, mu, w_r.astype(BF16), w_k.astype(BF16), w_v.astype(BF16), w1p, w2p, a1p, a2p, g1p, g2p,
              vecs, _head_ones(d)]
    tok = pl.BlockSpec((1, tm, d), lambda i, j: (i, j, 0))
    outs = pl.pallas_call(
        functools.partial(_rwkv_proj_kernel, tm=tm),
        out_shape=[jax.ShapeDtypeStruct((b, s, d), F32)] * 7,
        grid=(b, s // tm),
        in_specs=[tok, pl.BlockSpec((1,) + mod.shape[1:], lambda i, j: (i, 0, 0))]
                 + [_const_spec(c.shape) for c in consts],
        out_specs=[tok] * 7,
        scratch_shapes=[pltpu.VMEM((CONV_HALO + tm, d), F32)],
        compiler_params=_params(("arbitrary", "arbitrary")),
        name="rwkv_proj",
    )(x, mod, *consts)
    return outs


def _rwkv_scan_kernel(r_ref, k_ref, v_ref, kk_ref, a_ref, ld_ref, y_ref, ht_ref, *, nquad):
    c = pl.program_id(1)

    @pl.when(c == 0)
    def _():
        ht_ref[...] = jnp.zeros(ht_ref.shape, F32)

    row = lax.broadcasted_iota(jnp.int32, (QUAD, QUAD), 0) // HEAD
    col = lax.broadcasted_iota(jnp.int32, (QUAD, QUAD), 1) // HEAD
    blockmask = row == col
    t_i = lax.broadcasted_iota(jnp.int32, (CHUNK, QUAD), 0)
    s_i = lax.broadcasted_iota(jnp.int32, (CHUNK, QUAD), 1) % HEAD
    strict = s_i < t_i
    incl = s_i <= t_i
    blk16 = (s_i // 16) == (t_i // 16)
    off1 = ((s_i // 16) + 1 == (t_i // 16)) & ((t_i // 16) % 2 == 1)
    off2 = (s_i // 32 == 0) & (t_i // 32 == 1)
    eye_q = (s_i == t_i).astype(F32)
    ltri = (lax.broadcasted_iota(jnp.int32, (CHUNK, CHUNK), 1)
            <= lax.broadcasted_iota(jnp.int32, (CHUNK, CHUNK), 0)).astype(F32)

    def bd(x):
        x4 = jnp.concatenate([x, x, x, x], axis=0)
        return jnp.where(blockmask, x4, 0.0).astype(BF16)

    def mm(a, b_bd):
        return _dot(a.astype(BF16), b_bd)

    def mm_nt(a, b_bd):
        return _dot_nt(a.astype(BF16), b_bd)

    for q in range(nquad):
        sl = slice(q * QUAD, (q + 1) * QUAD)
        rc, kc, vc, kkc, ac, ldc = (ref[0, :, sl] for ref in (r_ref, k_ref, v_ref, kk_ref, a_ref, ld_ref))
        cum = _dot_f32(ltri, ldc)
        last = cum[CHUNK - 1:CHUNK, :]
        g_inv = jnp.exp(-cum)
        g_rem = jnp.exp(last - cum)
        at = -kkc * jnp.exp(cum - ldc)
        bsrc = kkc * ac
        bt_bd = bd(bsrc * g_inv)
        kt_bd = bd(kc * g_inv)
        rt = rc * jnp.exp(cum)
        v_bd = bd(vc)
        ht = ht_ref[q]
        ht_bd = bd(ht)

        a_ab = jnp.where(strict, mm_nt(at, bt_bd), 0.0)
        a_ak = jnp.where(strict, mm_nt(at, kt_bd), 0.0)
        a_rb = jnp.where(incl, mm_nt(rt, bt_bd), 0.0)
        a_rk = jnp.where(incl, mm_nt(rt, kt_bd), 0.0)

        ad = jnp.where(blk16, a_ab, 0.0)
        p = eye_q + ad
        a2 = mm(ad, bd(ad))
        p = p + mm(p, bd(a2))
        a4 = mm(a2, bd(a2))
        p = p + mm(p, bd(a4))
        a8 = mm(a4, bd(a4))
        p = p + mm(p, bd(a8))
        p_bd = bd(p)
        w32 = p + mm(mm(p, bd(jnp.where(off1, a_ab, 0.0))), p_bd)
        w32_bd = bd(w32)
        w = w32 + mm(mm(w32, bd(jnp.where(off2, a_ab, 0.0))), w32_bd)

        akv = mm(a_ak, v_bd)
        wa = mm(w, bd(at))
        uv = mm(w, bd(akv))
        u = mm_nt(wa, ht_bd) + uv
        u_bd = bd(u)
        y = mm_nt(rt, ht_bd) + mm(a_rb, u_bd) + mm(a_rk, v_bd)
        y_ref[0, :, sl] = y

        u_t = mm_nt(eye_q, u_bd)
        v_t = mm_nt(eye_q, v_bd)
        ht_ref[q] = jnp.exp(last) * ht + mm(u_t, bd(bsrc * g_rem)) + mm(v_t, bd(kc * g_rem))


def _rwkv_scan(r, k, v, kk, a, ld):
    b, s, d = r.shape
    nquad = d // QUAD
    blk = pl.BlockSpec((1, CHUNK, d), lambda i, c: (i, c, 0))
    return pl.pallas_call(
        functools.partial(_rwkv_scan_kernel, nquad=nquad),
        out_shape=jax.ShapeDtypeStruct((b, s, d), F32),
        grid=(b, s // CHUNK),
        in_specs=[blk] * 6,
        out_specs=blk,
        scratch_shapes=[pltpu.VMEM((nquad, HEAD, QUAD), F32)],
        compiler_params=_params(("arbitrary", "arbitrary")),
        name="rwkv_scan",
    )(r, k, v, kk, a, ld)


def _rwkv_out_kernel(x_ref, mod_ref, g_ref, y_ref, r_ref, k_ref, v_ref, gate_ref, vec_ref, ones_ref, wo_ref,
                     o_ref, *, ln_eps):
    x = x_ref[0]
    gt = mod_ref[0, 2:3, :]
    ln_g, ln_b, r_k = vec_ref[0:1, :], vec_ref[1:2, :], vec_ref[2:3, :]
    ones_bd = ones_ref[...]
    y = y_ref[0]
    mean = _group_sum(y, ones_bd) * (1.0 / HEAD)
    dlt = y - mean
    var = _group_sum(dlt * dlt, ones_bd) * (1.0 / HEAD)
    yn = dlt * lax.rsqrt(var + ln_eps) * ln_g + ln_b
    bonus = _group_sum(r_ref[0] * k_ref[0] * r_k, ones_bd) * v_ref[0]
    out = _dot(((yn + bonus) * gate_ref[0]).astype(BF16), wo_ref[...])
    o_ref[0] = x + gt * (_rms(out) * g_ref[1:2, :])


def _rwkv_out(x, mod, norm_g, y, r, k, v, gate, ln_g, ln_b, r_k, w_o, tm):
    b, s, d = x.shape
    vecs = jnp.stack([ln_g, ln_b, r_k.reshape(-1)], axis=0)
    tok = pl.BlockSpec((1, tm, d), lambda i, j: (i, j, 0))
    consts = [vecs, _head_ones(d), w_o.astype(BF16)]
    return pl.pallas_call(
        functools.partial(_rwkv_out_kernel, ln_eps=1e-5 * HEAD),
        out_shape=jax.ShapeDtypeStruct((b, s, d), F32),
        grid=(b, s // tm),
        in_specs=[tok, pl.BlockSpec((1,) + mod.shape[1:], lambda i, j: (i, 0, 0)), _const_spec(norm_g.shape)]
                 + [tok] * 5 + [_const_spec(c.shape) for c in consts],
        out_specs=tok,
        compiler_params=_params(("arbitrary", "arbitrary")),
        name="rwkv_out",
    )(x, mod, norm_g, y, r, k, v, gate, *consts)


def _router_kernel(x_ref, mod_ref, g_ref, wr_ref, h_out, route_out, *, n_exp):
    x = x_ref[0]
    sh, sc = mod_ref[0, 3:4, :], mod_ref[0, 4:5, :]
    h = (_rms(x) * g_ref[2:3, :]) * (1.0 + sc) + sh
    h_out[0] = h
    logits = _dot_f32(h, wr_ref[...])
    lane = lax.broadcasted_iota(jnp.int32, logits.shape, 1)
    neg = jnp.float32(-jnp.inf)
    logits = jnp.where(lane < n_exp, logits, neg)
    m1 = jnp.max(logits, axis=-1, keepdims=True)
    i1 = jnp.min(jnp.where(logits == m1, lane, LANE), axis=-1, keepdims=True)
    rest = jnp.where(lane == i1, neg, logits)
    m2 = jnp.max(rest, axis=-1, keepdims=True)
    i2 = jnp.min(jnp.where(rest == m2, lane, LANE), axis=-1, keepdims=True)
    w1 = 1.0 / (1.0 + jnp.exp(m2 - m1))
    w2 = 1.0 - w1
    route = jnp.where(lane == 0, i1.astype(F32), jnp.where(lane == 1, i2.astype(F32),
                      jnp.where(lane == 2, w1, jnp.where(lane == 3, w2, 0.0))))
    route_out[0] = route


def _router(x, mod, norm_g, w_router, tm):
    b, s, d = x.shape
    n_exp = w_router.shape[1]
    wr = _pad_cols(w_router, LANE)
    tok = pl.BlockSpec((1, tm, d), lambda i, j: (i, j, 0))
    return pl.pallas_call(
        functools.partial(_router_kernel, n_exp=n_exp),
        out_shape=[jax.ShapeDtypeStruct((b, s, d), F32), jax.ShapeDtypeStruct((b, s, LANE), F32)],
        grid=(b, s // tm),
        in_specs=[tok, pl.BlockSpec((1,) + mod.shape[1:], lambda i, j: (i, 0, 0)), _const_spec(norm_g.shape),
                  _const_spec(wr.shape)],
        out_specs=[tok, pl.BlockSpec((1, tm, LANE), lambda i, j: (i, j, 0))],
        compiler_params=_params(("arbitrary", "arbitrary")),
        name="moe_router",
    )(x, mod, norm_g, wr)


def _row_copy(src, src_row, dst, dst_row, sem):
    return pltpu.make_async_copy(src.at[pl.ds(src_row, 1)], dst.at[pl.ds(dst_row, 1)], sem)


def _dispatch_kernel(pos_ref, h_ref, init_ref, xs_ref, sem, *, tm):
    del init_ref

    def start(t, carry):
        for slot in range(TOP_K):
            _row_copy(h_ref, t, xs_ref, pos_ref[0, 0, TOP_K * t + slot], sem).start()
        return carry

    lax.fori_loop(0, tm, start, 0)

    def wait(t, carry):
        for _ in range(TOP_K):
            _row_copy(h_ref, 0, xs_ref, 0, sem).wait()
        return carry

    lax.fori_loop(0, tm, wait, 0)


def _dispatch(h2, pos, n_rows, tm):
    t, d = h2.shape
    nt = t // tm
    pos3 = pos.reshape(nt, 1, TOP_K * tm)
    init = jnp.zeros((n_rows, d), F32)
    return pl.pallas_call(
        functools.partial(_dispatch_kernel, tm=tm),
        out_shape=jax.ShapeDtypeStruct((n_rows, d), F32),
        grid=(nt,),
        in_specs=[
            pl.BlockSpec((1, 1, TOP_K * tm), lambda i: (i, 0, 0), memory_space=pltpu.SMEM),
            pl.BlockSpec((tm, d), lambda i: (i, 0)),
            pl.BlockSpec(memory_space=pl.ANY),
        ],
        out_specs=pl.BlockSpec(memory_space=pl.ANY),
        scratch_shapes=[pltpu.SemaphoreType.DMA(())],
        input_output_aliases={2: 0},
        compiler_params=_params(("arbitrary",)),
        name="moe_dispatch",
    )(pos3, h2, init)


def _expert_kernel(te_ref, nu_ref, xs_ref, wg_ref, wu_ref, wd_ref, ys_ref):
    i = pl.program_id(0)

    @pl.when(i < nu_ref[0])
    def _():
        h = xs_ref[...].astype(BF16)
        g = _dot(h, wg_ref[0])
        u = _dot(h, wu_ref[0])
        a = (g * jax.nn.sigmoid(g) * u).astype(BF16)
        ys_ref[...] = _dot(a, wd_ref[0])

    @pl.when(i >= nu_ref[0])
    def _():
        ys_ref[...] = jnp.zeros(ys_ref.shape, F32)


def _experts(xs, tile_expert, n_used, wg, wu, wd, tm):
    n_rows, d = xs.shape
    f = wg.shape[2]
    grid_spec = pltpu.PrefetchScalarGridSpec(
        num_scalar_prefetch=2,
        grid=(n_rows // tm,),
        in_specs=[
            pl.BlockSpec((tm, d), lambda i, te, nu: (i, 0)),
            pl.BlockSpec((1, d, f), lambda i, te, nu: (te[i], 0, 0), pipeline_mode=pl.Buffered(1)),
            pl.BlockSpec((1, d, f), lambda i, te, nu: (te[i], 0, 0), pipeline_mode=pl.Buffered(1)),
            pl.BlockSpec((1, f, d), lambda i, te, nu: (te[i], 0, 0), pipeline_mode=pl.Buffered(1)),
        ],
        out_specs=pl.BlockSpec((tm, d), lambda i, te, nu: (i, 0)),
    )
    return pl.pallas_call(
        _expert_kernel,
        out_shape=jax.ShapeDtypeStruct((n_rows, d), F32),
        grid_spec=grid_spec,
        compiler_params=_params(("arbitrary",)),
        name="moe_experts",
    )(tile_expert, n_used, xs, wg.astype(BF16), wu.astype(BF16), wd.astype(BF16))


def _combine_kernel(pos_ref, x_ref, mod_ref, g_ref, route_ref, ys_ref, o_ref, buf, sem, *, tm):
    def start(t, carry):
        for slot in range(TOP_K):
            _row_copy(ys_ref, pos_ref[0, 0, TOP_K * t + slot], buf.at[slot], t, sem).start()
        return carry

    lax.fori_loop(0, tm, start, 0)

    def wait(t, carry):
        for slot in range(TOP_K):
            _row_copy(ys_ref, 0, buf.at[slot], 0, sem).wait()
        return carry

    lax.fori_loop(0, tm, wait, 0)

    x = x_ref[0]
    gt = mod_ref[0, 5:6, :]
    route = route_ref[0]
    y = route[:, 2:3] * buf[0] + route[:, 3:4] * buf[1]
    o_ref[0] = x + gt * (_rms(y) * g_ref[3:4, :])


def _combine(x, mod, norm_g, route, ys, pos, tm):
    b, s, d = x.shape
    nj = s // tm
    pos3 = pos.reshape(b * nj, 1, TOP_K * tm)
    tok = pl.BlockSpec((1, tm, d), lambda i, j: (i, j, 0))
    return pl.pallas_call(
        functools.partial(_combine_kernel, tm=tm),
        out_shape=jax.ShapeDtypeStruct((b, s, d), F32),
        grid=(b, nj),
        in_specs=[
            pl.BlockSpec((1, 1, TOP_K * tm), lambda i, j: (i * nj + j, 0, 0), memory_space=pltpu.SMEM),
            tok,
            pl.BlockSpec((1,) + mod.shape[1:], lambda i, j: (i, 0, 0)),
            _const_spec(norm_g.shape),
            pl.BlockSpec((1, tm, LANE), lambda i, j: (i, j, 0)),
            pl.BlockSpec(memory_space=pl.ANY),
        ],
        out_specs=tok,
        scratch_shapes=[pltpu.VMEM((TOP_K, tm, d), F32), pltpu.SemaphoreType.DMA(())],
        compiler_params=_params(("arbitrary", "arbitrary")),
        name="moe_combine",
    )(pos3, x, mod, norm_g, route, ys)


def _route_positions(idx, n_exp, tm):
    flat = idx.reshape(-1)
    onehot = (flat[:, None] == jnp.arange(n_exp, dtype=jnp.int32)[None, :]).astype(jnp.int32)
    csum = jnp.cumsum(onehot, axis=0)
    rank = jnp.sum(onehot * csum, axis=1) - 1
    counts = csum[-1]
    tiles = (counts + tm - 1) // tm
    tile_end = jnp.cumsum(tiles)
    tile_start = tile_end - tiles
    pos = tile_start[flat] * tm + rank
    n_tiles = flat.shape[0] // tm + n_exp
    tile_expert = jnp.minimum(jnp.searchsorted(tile_end, jnp.arange(n_tiles, dtype=jnp.int32), side="right"),
                              n_exp - 1).astype(jnp.int32)
    return pos.astype(jnp.int32), tile_expert, tile_end[-1:].astype(jnp.int32), n_tiles


def _moe(x, mod, norm_g, w_router, wg, wu, wd, tm, tm_e):
    b, s, d = x.shape
    n_exp = w_router.shape[1]
    h2, route = _router(x, mod, norm_g, w_router, tm)
    idx = route[..., 0:TOP_K].astype(jnp.int32).reshape(b * s, TOP_K)
    pos, tile_expert, n_used, n_tiles = _route_positions(idx, n_exp, tm_e)
    xs = _dispatch(h2.reshape(b * s, d), pos, n_tiles * tm_e, tm)
    ys = _experts(xs, tile_expert, n_used, wg, wu, wd, tm_e)
    return _combine(x, mod, norm_g, route, ys, pos, tm)


def _tile(s, want):
    t = min(want, s)
    assert s % t == 0 and t % CHUNK == 0
    return t


def kernel(x, c, ada_w, ada_b, norm_g, mix_w_in, conv_w, pool_w, pool_scale, mix_w_out, ffn_w_gate, ffn_w_up, ffn_w_down, rwkv_mu, rwkv_w_r, rwkv_w_k, rwkv_w_v, rwkv_w_o, rwkv_w0, rwkv_w1, rwkv_w2, rwkv_a0, rwkv_a1, rwkv_a2, rwkv_g1, rwkv_g2, rwkv_k_k, rwkv_k_a, rwkv_r_k, rwkv_ln_g, rwkv_ln_b, moe_router, moe_w_gate, moe_w_up, moe_w_down):
    depth = ada_w.shape[0]
    s = x.shape[1]
    tm = _tile(s, 512)
    mods = _ada(c, ada_w, ada_b)
    for layer in range(depth):
        mod, ng, i = mods[layer], norm_g[layer], layer // 2
        if layer % 2 == 0:
            x = _mixer(x, mod, ng, mix_w_in[i], conv_w[i], pool_w[i], pool_scale[i], mix_w_out[i], tm)
            x = _ffn(x, mod, ng, ffn_w_gate[i], ffn_w_up[i], ffn_w_down[i], tm)
        else:
            r, k, v, kk, a, ld, gate = _rwkv_proj(
                x, mod, ng, rwkv_mu[i], rwkv_w_r[i], rwkv_w_k[i], rwkv_w_v[i], rwkv_w0[i], rwkv_w1[i], rwkv_w2[i],
                rwkv_a0[i], rwkv_a1[i], rwkv_a2[i], rwkv_g1[i], rwkv_g2[i], rwkv_k_k[i], rwkv_k_a[i], tm)
            y = _rwkv_scan(r, k, v, kk, a, ld)
            x = _rwkv_out(x, mod, ng, y, r, k, v, gate, rwkv_ln_g[i], rwkv_ln_b[i], rwkv_r_k[i], rwkv_w_o[i], tm)
            x = _moe(x, mod, ng, moe_router[i], moe_w_gate[i], moe_w_up[i], moe_w_down[i], tm, tm)
    return x
```

```python
import functools

import jax
import jax.numpy as jnp
from jax import lax
from jax.experimental import pallas as pl
from jax.experimental.pallas import tpu as pltpu

F32 = jnp.float32
BF16 = jnp.bfloat16

RMS_EPS = 1e-6
HEAD = 64
QUAD = 4 * HEAD
CHUNK = 64
POOL_WINDOWS = (2, 4, 8, 16)
MAX_WIN = 16
CONV_HALO = 8
LANE = 128
TOP_K = 2
VMEM_LIMIT = 56 * 1024 * 1024


def _dot(a, b):
    return jnp.dot(a, b, preferred_element_type=F32)


def _dot_nt(a, b):
    return lax.dot_general(a, b, (((1,), (1,)), ((), ())), preferred_element_type=F32)


def _dot_f32(a, b):
    return jnp.dot(a, b, preferred_element_type=F32, precision=lax.Precision.HIGHEST)


def _rms(x):
    return x * lax.rsqrt(jnp.mean(x * x, axis=-1, keepdims=True) + RMS_EPS)


def _params(sem):
    return pltpu.CompilerParams(dimension_semantics=sem, vmem_limit_bytes=VMEM_LIMIT)


def _const_spec(shape):
    nd = len(shape)
    return pl.BlockSpec(shape, lambda *_: (0,) * nd, pipeline_mode=pl.Buffered(1))


def _ada_kernel(c_ref, w_ref, b_ref, o_ref):
    c = c_ref[...]
    cond = c * jax.nn.sigmoid(c)
    o_ref[0] = _dot_f32(cond, w_ref[0]) + b_ref[0]


def _ada(c, ada_w, ada_b):
    depth, d, d6 = ada_w.shape
    b = c.shape[0]
    nblk = d6 // d
    out = pl.pallas_call(
        _ada_kernel,
        out_shape=jax.ShapeDtypeStruct((depth, b, d6), F32),
        grid=(depth, nblk),
        in_specs=[
            pl.BlockSpec((b, d), lambda l, n: (0, 0)),
            pl.BlockSpec((1, d, d), lambda l, n: (l, 0, n)),
            pl.BlockSpec((1, 1, d), lambda l, n: (l, 0, n)),
        ],
        out_specs=pl.BlockSpec((1, b, d), lambda l, n: (l, 0, n)),
        compiler_params=_params(("arbitrary", "arbitrary")),
        name="ada_mod",
    )(c, ada_w, ada_b.reshape(depth, 1, d6))
    return out.reshape(depth, b, nblk, d)


def _mix_kernel(x_ref, mod_ref, g_ref, w_in_ref, conv_w_ref, pool_w_ref, pool_s_ref, w_out_ref,
                o_ref, u_ext, v_ext, u_halo, v_halo, *, tm, dc, gc):
    j = pl.program_id(1)

    @pl.when(j == 0)
    def _():
        u_halo[...] = jnp.zeros(u_halo.shape, F32)
        v_halo[...] = jnp.zeros(v_halo.shape, F32)

    u_ext[0:CONV_HALO, :] = u_halo[...]
    v_ext[0:MAX_WIN, :] = v_halo[...]

    x = x_ref[0]
    sh, sc, gt = mod_ref[0, 0:1, :], mod_ref[0, 1:2, :], mod_ref[0, 2:3, :]
    h = (_rms(x) * g_ref[0:1, :]) * (1.0 + sc) + sh
    z = _dot(h.astype(BF16), w_in_ref[...])
    b_gate, c_gate, v_conv, v_pool = z[:, 0:dc], z[:, dc:2 * dc], z[:, 2 * dc:3 * dc], z[:, 3 * dc:]

    u = c_gate * v_conv
    u_ext[CONV_HALO:CONV_HALO + tm, :] = u
    um1 = u_ext[CONV_HALO - 1:CONV_HALO - 1 + tm, :]
    um2 = u_ext[CONV_HALO - 2:CONV_HALO - 2 + tm, :]
    y_conv = b_gate * (um2 * conv_w_ref[0:1, :] + um1 * conv_w_ref[1:2, :] + u * conv_w_ref[2:3, :])
    u_halo[...] = u[tm - CONV_HALO:tm, :]

    v_ext[MAX_WIN:MAX_WIN + tm, :] = v_pool
    pos = j * tm + lax.broadcasted_iota(jnp.int32, (tm, 1), 0)
    pieces = [y_conv.astype(BF16)]
    for gi, win in enumerate(POOL_WINDOWS):
        lo, hi = gi * gc, (gi + 1) * gc
        tot = v_pool[:, lo:hi]
        for d in range(1, win):
            tot = tot + v_ext[MAX_WIN - d:MAX_WIN - d + tm, lo:hi]
        cnt = jnp.minimum(pos + 1, win).astype(F32)
        p = tot / cnt - v_pool[:, lo:hi]
        yp = _dot(p.astype(BF16), pool_w_ref[gi]) * pool_s_ref[0:1, lo:hi]
        pieces.append(yp.astype(BF16))
    v_halo[...] = v_pool[tm - MAX_WIN:tm, :]

    y = _dot(jnp.concatenate(pieces, axis=-1), w_out_ref[...])
    o_ref[0] = x + gt * (_rms(y) * g_ref[1:2, :])


def _mixer(x, mod, norm_g, w_in, conv_w, pool_w, pool_scale, w_out, tm):
    b, s, d = x.shape
    dc = conv_w.shape[1]
    ng, gc = pool_w.shape[0], pool_w.shape[1]
    kern = functools.partial(_mix_kernel, tm=tm, dc=dc, gc=gc)
    return pl.pallas_call(
        kern,
        out_shape=jax.ShapeDtypeStruct((b, s, d), F32),
        grid=(b, s // tm),
        in_specs=[
            pl.BlockSpec((1, tm, d), lambda i, j: (i, j, 0)),
            pl.BlockSpec((1,) + mod.shape[1:], lambda i, j: (i, 0, 0)),
            _const_spec(norm_g.shape),
            _const_spec(w_in.shape),
            _const_spec(conv_w.shape),
            _const_spec(pool_w.shape),
            _const_spec((1, ng * gc)),
            _const_spec(w_out.shape),
        ],
        out_specs=pl.BlockSpec((1, tm, d), lambda i, j: (i, j, 0)),
        scratch_shapes=[pltpu.VMEM((CONV_HALO + tm, dc), F32), pltpu.VMEM((MAX_WIN + tm, ng * gc), F32),
                        pltpu.VMEM((CONV_HALO, dc), F32), pltpu.VMEM((MAX_WIN, ng * gc), F32)],
        compiler_params=_params(("arbitrary", "arbitrary")),
        name="conv_pool_mixer",
    )(x, mod, norm_g, w_in.astype(BF16), conv_w, pool_w.astype(BF16), pool_scale.reshape(1, -1),
      w_out.astype(BF16))


def _ffn_kernel(x_ref, mod_ref, g_ref, wg_ref, wu_ref, wd_ref, o_ref):
    x = x_ref[0]
    sh, sc, gt = mod_ref[0, 3:4, :], mod_ref[0, 4:5, :], mod_ref[0, 5:6, :]
    h = ((_rms(x) * g_ref[2:3, :]) * (1.0 + sc) + sh).astype(BF16)
    g = _dot(h, wg_ref[...])
    u = _dot(h, wu_ref[...])
    a = (g * jax.nn.sigmoid(g) * u).astype(BF16)
    y = _dot(a, wd_ref[...])
    o_ref[0] = x + gt * (_rms(y) * g_ref[3:4, :])


def _ffn(x, mod, norm_g, wg, wu, wd, tm):
    b, s, d = x.shape
    return pl.pallas_call(
        _ffn_kernel,
        out_shape=jax.ShapeDtypeStruct((b, s, d), F32),
        grid=(b, s // tm),
        in_specs=[
            pl.BlockSpec((1, tm, d), lambda i, j: (i, j, 0)),
            pl.BlockSpec((1,) + mod.shape[1:], lambda i, j: (i, 0, 0)),
            _const_spec(norm_g.shape),
            _const_spec(wg.shape),
            _const_spec(wu.shape),
            _const_spec(wd.shape),
        ],
        out_specs=pl.BlockSpec((1, tm, d), lambda i, j: (i, j, 0)),
        compiler_params=_params(("arbitrary", "arbitrary")),
        name="dense_swiglu",
    )(x, mod, norm_g, wg.astype(BF16), wu.astype(BF16), wd.astype(BF16))


def _group_sum(x, ones_bd):
    return _dot(x.astype(BF16), ones_bd)


def _rwkv_proj_kernel(x_ref, mod_ref, g_ref, mu_ref, wr_ref, wk_ref, wv_ref, w1_ref, w2_ref, a1_ref, a2_ref,
                      g1_ref, g2_ref, vec_ref, ones_ref,
                      r_out, k_out, v_out, kk_out, a_out, ld_out, g_out, h_ext, h_halo, *, tm):
    j = pl.program_id(1)

    @pl.when(j == 0)
    def _():
        h_halo[...] = jnp.zeros(h_halo.shape, F32)

    x = x_ref[0]
    sh, sc = mod_ref[0, 0:1, :], mod_ref[0, 1:2, :]
    h = (_rms(x) * g_ref[0:1, :]) * (1.0 + sc) + sh
    h_ext[0:CONV_HALO, :] = h_halo[...]
    h_ext[CONV_HALO:CONV_HALO + tm, :] = h
    h_prev = h_ext[CONV_HALO - 1:CONV_HALO - 1 + tm, :]
    h_halo[...] = h[tm - CONV_HALO:tm, :]
    xx = h_prev - h

    def mixed(i):
        return (h + xx * mu_ref[i:i + 1, :]).astype(BF16)

    w0, a0, k_k, k_a = vec_ref[0:1, :], vec_ref[1:2, :], vec_ref[2:3, :], vec_ref[3:4, :]
    r = _dot(mixed(0), wr_ref[...])
    w_lora = _dot(jnp.tanh(_dot(mixed(1), w1_ref[...])).astype(BF16), w2_ref[...])
    k = _dot(mixed(2), wk_ref[...])
    v = _dot(mixed(3), wv_ref[...])
    a_lora = _dot(_dot(mixed(4), a1_ref[...]).astype(BF16), a2_ref[...])
    gate = _dot(jax.nn.sigmoid(_dot(mixed(5), g1_ref[...])).astype(BF16), g2_ref[...])

    w_log = -jax.nn.softplus(-(w0 + w_lora)) - 0.5
    a = jax.nn.sigmoid(a0 + a_lora)
    kk = k * k_k
    nrm = jnp.sqrt(_group_sum(kk * kk, ones_ref[...]))
    kk = kk / jnp.maximum(nrm, 1e-12)

    r_out[0] = r
    k_out[0] = k * (1.0 + (a - 1.0) * k_a)
    v_out[0] = v
    kk_out[0] = kk
    a_out[0] = a
    ld_out[0] = -jnp.exp(w_log)
    g_out[0] = gate


def _pad_cols(w, n):
    return jnp.pad(w, ((0, 0), (0, n - w.shape[1])))


def _pad_rows(w, n):
    return jnp.pad(w, ((0, n - w.shape[0]), (0, 0)))


def _ceil_to(n, m):
    return -(-n // m) * m


def _head_ones(d):
    idx = jnp.arange(d) // HEAD
    return (idx[:, None] == idx[None, :]).astype(BF16)


def _rwkv_proj(x, mod, norm_g, mu, w_r, w_k, w_v, w0, w1, w2, a0, a1, a2, g1, g2, k_k, k_a, tm):
    b, s, d = x.shape
    lw, la, lg = (_ceil_to(w.shape[1], LANE) for w in (w1, a1, g1))
    w1p, w2p = _pad_cols(w1, lw).astype(BF16), _pad_rows(w2, lw).astype(BF16)
    a1p, a2p = _pad_cols(a1, la).astype(BF16), _pad_rows(a2, la).astype(BF16)
    g1p, g2p = _pad_cols(g1, lg).astype(BF16), _pad_rows(g2, lg).astype(BF16)
    vecs = jnp.stack([w0, a0, k_k, k_a], axis=0)
    consts = [norm_g, mu, w_r.astype(BF16), w_k.astype(BF16), w_v.astype(BF16), w1p, w2p, a1p, a2p, g1p, g2p,
              vecs, _head_ones(d)]
    tok = pl.BlockSpec((1, tm, d), lambda i, j: (i, j, 0))
    outs = pl.pallas_call(
        functools.partial(_rwkv_proj_kernel, tm=tm),
        out_shape=[jax.ShapeDtypeStruct((b, s, d), F32)] * 7,
        grid=(b, s // tm),
        in_specs=[tok, pl.BlockSpec((1,) + mod.shape[1:], lambda i, j: (i, 0, 0))]
                 + [_const_spec(c.shape) for c in consts],
        out_specs=[tok] * 7,
        scratch_shapes=[pltpu.VMEM((CONV_HALO + tm, d), F32), pltpu.VMEM((CONV_HALO, d), F32)],
        compiler_params=_params(("arbitrary", "arbitrary")),
        name="rwkv_proj",
    )(x, mod, *consts)
    return outs


PAIR = 2


def _rwkv_scan_kernel(r_ref, k_ref, v_ref, kk_ref, a_ref, ld_ref, rk_ref, y_ref, bonus_ref,
                      ht_s, lhs1_s, arb_s, ark_s, uv_s, v_s, bk_s, gl_s, *, nquad):
    step = pl.program_id(1)

    @pl.when(step == 0)
    def _():
        for ref in (ht_s, lhs1_s, arb_s, ark_s, uv_s, v_s, bk_s, gl_s):
            ref[...] = jnp.zeros(ref.shape, ref.dtype)

    row = lax.broadcasted_iota(jnp.int32, (QUAD, QUAD), 0) // HEAD
    col = lax.broadcasted_iota(jnp.int32, (QUAD, QUAD), 1) // HEAD
    blockmask = row == col
    t_i = lax.broadcasted_iota(jnp.int32, (CHUNK, QUAD), 0)
    s_i = lax.broadcasted_iota(jnp.int32, (CHUNK, QUAD), 1) % HEAD
    strict = s_i < t_i
    incl = s_i <= t_i
    blk16 = (s_i // 16) == (t_i // 16)
    off1 = ((s_i // 16) + 1 == (t_i // 16)) & ((t_i // 16) % 2 == 1)
    off2 = (s_i // 32 == 0) & (t_i // 32 == 1)
    eye_q = (s_i == t_i).astype(F32)
    ltri = (lax.broadcasted_iota(jnp.int32, (CHUNK, CHUNK), 1)
            <= lax.broadcasted_iota(jnp.int32, (CHUNK, CHUNK), 0)).astype(BF16)

    def bd(x):
        x4 = jnp.concatenate([x, x, x, x], axis=0)
        return jnp.where(blockmask, x4, 0.0).astype(BF16)

    def mm(a, b_bd):
        return _dot(a.astype(BF16), b_bd)

    def stack(a, b):
        return jnp.concatenate([a, b], axis=0).astype(BF16)

    probs = [(c, q) for c in range(PAIR) for q in range(nquad)]
    P = {}
    hts = [ht_s[q] for q in range(nquad)]
    S = {}

    def p1_prep():
        for c in range(PAIR):
            rows = slice(c * CHUNK, (c + 1) * CHUNK)
            ld_c = ld_ref[0, rows, :]
            hi = ld_c.astype(BF16)
            rem = ld_c - hi.astype(F32)
            mid = rem.astype(BF16)
            lo = (rem - mid.astype(F32)).astype(BF16)
            cum_c = _dot(ltri, hi) + _dot(ltri, mid) + _dot(ltri, lo)
            for q in range(nquad):
                sl = slice(q * QUAD, (q + 1) * QUAD)
                rc, kc, vc, kkc, ac = (ref[0, rows, sl] for ref in (r_ref, k_ref, v_ref, kk_ref, a_ref))
                cum, ldc = cum_c[:, sl], ld_c[:, sl]
                last = cum[CHUNK - 1:CHUNK, :]
                g_inv = jnp.exp(-cum)
                g_rem = jnp.exp(last - cum)
                bsrc = kkc * ac
                at = -kkc * jnp.exp(cum - ldc)
                rt = rc * jnp.exp(cum)
                P[c, q] = dict(at=at, rt=rt, v=vc, bt_bd=bd(bsrc * g_inv), kt_bd=bd(kc * g_inv),
                               rtk=(rt * rk_ref[0:1, sl]).astype(BF16),
                               bk=stack(bsrc * g_rem, kc * g_rem), gl=jnp.exp(last))

    def p1_amat():
        for p in probs:
            d = P[p]
            lhs = stack(d["at"], d["rt"])
            ab = _dot_nt(lhs, d.pop("bt_bd"))
            ak = _dot_nt(jnp.concatenate([lhs, d.pop("rtk")], axis=0), d.pop("kt_bd"))
            d["a_ab"] = jnp.where(strict, ab[0:CHUNK], 0.0)
            d["a_rb"] = jnp.where(incl, ab[CHUNK:], 0.0)
            d["a_ak"] = jnp.where(strict, ak[0:CHUNK], 0.0)
            d["a_rk"] = jnp.where(incl, ak[CHUNK:2 * CHUNK], 0.0)
            d["bonus"] = jnp.where(s_i == t_i, ak[2 * CHUNK:], 0.0)

    def p1_n1():
        for p in probs:
            d = P[p]
            ad = jnp.where(blk16, d["a_ab"], 0.0)
            d["p"] = eye_q + ad
            d["pw"] = mm(ad, bd(ad))

    def p1_n2():
        for p in probs:
            d = P[p]
            both = _dot(stack(d["p"], d["pw"]), bd(d["pw"]))
            d["p"] = d["p"] + both[0:CHUNK]
            d["pw"] = both[CHUNK:]

    def p1_n4():
        for p in probs:
            d = P[p]
            d["p"] = d["p"] + mm(d["p"], bd(d.pop("pw")))

    def p1_m1(mask):
        def run():
            for p in probs:
                d = P[p]
                d["p_bd"] = bd(d["p"])
                d["t"] = mm(d["p"], bd(jnp.where(mask, d["a_ab"], 0.0)))
        return run

    def p1_m2():
        for p in probs:
            d = P[p]
            d["p"] = d["p"] + mm(d.pop("t"), d.pop("p_bd"))

    def p1_akv():
        for (c, q) in probs:
            d = P[c, q]
            both = _dot(stack(d.pop("a_ak"), d.pop("bonus")), bd(d["v"]))
            d["akv"] = both[0:CHUNK]
            bonus_ref[0, c * CHUNK:(c + 1) * CHUNK, q * QUAD:(q + 1) * QUAD] = both[CHUNK:]

    def p1_solve():
        for p in probs:
            d = P[p]
            w = d.pop("p").astype(BF16)
            d["wa"] = _dot(w, bd(d.pop("at")))
            d["uv"] = _dot(w, bd(d.pop("akv")))

    wr_base = (step % 2) * len(probs)
    rd_base = len(probs) - wr_base

    def p1_store():
        for n, p in enumerate(probs):
            d, i = P[p], wr_base + n
            lhs1_s[i] = stack(d["wa"], d["rt"])
            arb_s[i] = d["a_rb"].astype(BF16)
            ark_s[i] = d["a_rk"].astype(BF16)
            uv_s[i] = d["uv"]
            v_s[i] = d["v"]
            bk_s[i] = d["bk"]
            gl_s[i] = jnp.broadcast_to(d["gl"], (8, QUAD))

    def p2_a(c):
        def run():
            for q in range(nquad):
                i = rd_base + c * nquad + q
                uy = _dot_nt(lhs1_s[i], hts[q].astype(BF16))
                S[q] = dict(u=uy[0:CHUNK] + uv_s[i], y1=uy[CHUNK:])
        return run

    def p2_b(c):
        def run():
            rows = slice(c * CHUNK, (c + 1) * CHUNK)
            for q in range(nquad):
                i = rd_base + c * nquad + q
                d = S.pop(q)
                u, vc = d["u"], v_s[i]
                y_ref[0, rows, q * QUAD:(q + 1) * QUAD] = d["y1"] + _dot(arb_s[i], bd(u)) + _dot(ark_s[i], bd(vc))
                uv_t = jnp.concatenate([u, vc], axis=0).T.astype(BF16)
                upd = _dot(uv_t, bk_s[i])
                hts[q] = gl_s[i][0:1, :] * hts[q] + jnp.where(blockmask, upd, 0.0)
        return run

    def p2_store():
        for q in range(nquad):
            ht_s[q] = hts[q]

    for stage in (p1_prep, p2_a(0), p1_amat, p2_b(0), p1_n1, p2_a(1), p1_n2, p2_b(1), p2_store, p1_n2, p1_n4,
                  p1_m1(off1), p1_m2, p1_m1(off2), p1_m2, p1_akv, p1_solve, p1_store):
        stage()


def _rwkv_scan(r, k, v, kk, a, ld, r_k):
    b, s, d = r.shape
    nquad = d // QUAD
    span = PAIR * CHUNK
    npair = s // span
    nprob = 2 * PAIR * nquad
    blk_in = pl.BlockSpec((1, span, d), lambda i, c: (i, jnp.minimum(c, npair - 1), 0))
    blk_out = pl.BlockSpec((1, span, d), lambda i, c: (i, jnp.maximum(c - 1, 0), 0))
    return pl.pallas_call(
        functools.partial(_rwkv_scan_kernel, nquad=nquad),
        out_shape=[jax.ShapeDtypeStruct((b, s, d), F32)] * 2,
        grid=(b, npair + 1),
        in_specs=[blk_in] * 6 + [_const_spec((1, d))],
        out_specs=[blk_out, blk_in],
        scratch_shapes=[
            pltpu.VMEM((nquad, QUAD, QUAD), F32),
            pltpu.VMEM((nprob, 2 * CHUNK, QUAD), BF16),
            pltpu.VMEM((nprob, CHUNK, QUAD), BF16),
            pltpu.VMEM((nprob, CHUNK, QUAD), BF16),
            pltpu.VMEM((nprob, CHUNK, QUAD), F32),
            pltpu.VMEM((nprob, CHUNK, QUAD), F32),
            pltpu.VMEM((nprob, 2 * CHUNK, QUAD), BF16),
            pltpu.VMEM((nprob, 8, QUAD), F32),
        ],
        compiler_params=_params(("arbitrary", "arbitrary")),
        name="rwkv_scan",
    )(r, k, v, kk, a, ld, r_k.reshape(1, d))


def _rwkv_out_kernel(x_ref, mod_ref, g_ref, y_ref, bonus_ref, gate_ref, vec_ref, ones_ref, wo_ref,
                     o_ref, *, ln_eps):
    x = x_ref[0]
    gt = mod_ref[0, 2:3, :]
    ln_g, ln_b = vec_ref[0:1, :], vec_ref[1:2, :]
    ones_bd = ones_ref[...]
    y = y_ref[0]
    mean = _group_sum(y, ones_bd) * (1.0 / HEAD)
    dlt = y - mean
    var = _group_sum(dlt * dlt, ones_bd) * (1.0 / HEAD)
    yn = dlt * lax.rsqrt(var + ln_eps) * ln_g + ln_b
    out = _dot(((yn + bonus_ref[0]) * gate_ref[0]).astype(BF16), wo_ref[...])
    o_ref[0] = x + gt * (_rms(out) * g_ref[1:2, :])


def _rwkv_out(x, mod, norm_g, y, bonus, gate, ln_g, ln_b, w_o, tm):
    b, s, d = x.shape
    vecs = jnp.stack([ln_g, ln_b], axis=0)
    tok = pl.BlockSpec((1, tm, d), lambda i, j: (i, j, 0))
    consts = [vecs, _head_ones(d), w_o.astype(BF16)]
    return pl.pallas_call(
        functools.partial(_rwkv_out_kernel, ln_eps=1e-5 * HEAD),
        out_shape=jax.ShapeDtypeStruct((b, s, d), F32),
        grid=(b, s // tm),
        in_specs=[tok, pl.BlockSpec((1,) + mod.shape[1:], lambda i, j: (i, 0, 0)), _const_spec(norm_g.shape)]
                 + [tok] * 3 + [_const_spec(c.shape) for c in consts],
        out_specs=tok,
        compiler_params=_params(("arbitrary", "arbitrary")),
        name="rwkv_out",
    )(x, mod, norm_g, y, bonus, gate, *consts)


def _router_kernel(x_ref, mod_ref, g_ref, wr_ref, h_out, route_out, *, n_exp):
    x = x_ref[0]
    sh, sc = mod_ref[0, 3:4, :], mod_ref[0, 4:5, :]
    h = (_rms(x) * g_ref[2:3, :]) * (1.0 + sc) + sh
    h_out[0] = h
    logits = _dot_f32(h, wr_ref[...])
    lane = lax.broadcasted_iota(jnp.int32, logits.shape, 1)
    neg = jnp.float32(-jnp.inf)
    logits = jnp.where(lane < n_exp, logits, neg)
    m1 = jnp.max(logits, axis=-1, keepdims=True)
    i1 = jnp.min(jnp.where(logits == m1, lane, LANE), axis=-1, keepdims=True)
    rest = jnp.where(lane == i1, neg, logits)
    m2 = jnp.max(rest, axis=-1, keepdims=True)
    i2 = jnp.min(jnp.where(rest == m2, lane, LANE), axis=-1, keepdims=True)
    w1 = 1.0 / (1.0 + jnp.exp(m2 - m1))
    w2 = 1.0 - w1
    route = jnp.where(lane == 0, i1.astype(F32), jnp.where(lane == 1, i2.astype(F32),
                      jnp.where(lane == 2, w1, jnp.where(lane == 3, w2, 0.0))))
    route_out[0] = route


def _router(x, mod, norm_g, w_router, tm):
    b, s, d = x.shape
    n_exp = w_router.shape[1]
    wr = _pad_cols(w_router, LANE)
    tok = pl.BlockSpec((1, tm, d), lambda i, j: (i, j, 0))
    return pl.pallas_call(
        functools.partial(_router_kernel, n_exp=n_exp),
        out_shape=[jax.ShapeDtypeStruct((b, s, d), F32), jax.ShapeDtypeStruct((b, s, LANE), F32)],
        grid=(b, s // tm),
        in_specs=[tok, pl.BlockSpec((1,) + mod.shape[1:], lambda i, j: (i, 0, 0)), _const_spec(norm_g.shape),
                  _const_spec(wr.shape)],
        out_specs=[tok, pl.BlockSpec((1, tm, LANE), lambda i, j: (i, j, 0))],
        compiler_params=_params(("arbitrary", "arbitrary")),
        name="moe_router",
    )(x, mod, norm_g, wr)


def _row_copy(src, src_row, dst, dst_row, sem):
    return pltpu.make_async_copy(src.at[pl.ds(src_row, 1)], dst.at[pl.ds(dst_row, 1)], sem)


def _expert_kernel(te_ref, src_cur, src_next, dst_prev, dst_cur, h_ref, wg_ref, wu_ref, wd_ref, out_ref,
                   xbuf, ybuf, gsem, ssem, *, tm):
    del te_ref
    i = pl.program_id(0)
    last = pl.num_programs(0) - 1
    cur, nxt = i % 2, (i + 1) % 2

    def gather(idx_ref, slot):
        return [_row_copy(h_ref, idx_ref[0, 0, r], xbuf.at[slot], r, gsem) for r in range(tm)]

    def scatter(idx_ref, slot):
        return [_row_copy(ybuf.at[slot], r, out_ref, idx_ref[0, 0, r], ssem) for r in range(tm)]

    def run(copies):
        for cp in copies:
            cp.start()
        for cp in copies:
            cp.wait()

    @pl.when(i == 0)
    def _():
        ybuf[...] = jnp.zeros(ybuf.shape, F32)
        run(gather(src_cur, 0))

    copies = gather(src_next, nxt) + scatter(dst_prev, nxt)
    for cp in copies:
        cp.start()
    h = xbuf[cur].astype(BF16)
    g = _dot(h, wg_ref[0])
    u = _dot(h, wu_ref[0])
    a = (g * jax.nn.sigmoid(g) * u).astype(BF16)
    ybuf[cur] = _dot(a, wd_ref[0])
    for cp in copies:
        cp.wait()

    @pl.when(i == last)
    def _():
        run(scatter(dst_cur, cur))


def _experts(h2, tile_expert, src, dst, n_out, wg, wu, wd, tm):
    t, d = h2.shape
    f = wg.shape[2]
    n_tiles = src.shape[0] // tm
    src3, dst3 = src.reshape(n_tiles, 1, tm), dst.reshape(n_tiles, 1, tm)

    def idx_spec(fn):
        return pl.BlockSpec((1, 1, tm), lambda i, te: (fn(i), 0, 0), memory_space=pltpu.SMEM)

    def w_spec(shape):
        return pl.BlockSpec((1,) + shape, lambda i, te: (te[i], 0, 0), pipeline_mode=pl.Buffered(1))

    grid_spec = pltpu.PrefetchScalarGridSpec(
        num_scalar_prefetch=1,
        grid=(n_tiles,),
        in_specs=[
            idx_spec(lambda i: i),
            idx_spec(lambda i: jnp.minimum(i + 1, n_tiles - 1)),
            idx_spec(lambda i: jnp.maximum(i - 1, 0)),
            idx_spec(lambda i: i),
            pl.BlockSpec(memory_space=pl.ANY),
            w_spec((d, f)), w_spec((d, f)), w_spec((f, d)),
        ],
        out_specs=pl.BlockSpec(memory_space=pl.ANY),
        scratch_shapes=[pltpu.VMEM((2, tm, d), F32), pltpu.VMEM((2, tm, d), F32),
                        pltpu.SemaphoreType.DMA(()), pltpu.SemaphoreType.DMA(())],
    )
    return pl.pallas_call(
        functools.partial(_expert_kernel, tm=tm),
        out_shape=jax.ShapeDtypeStruct((n_out, d), F32),
        grid_spec=grid_spec,
        compiler_params=_params(("arbitrary",)),
        name="moe_experts",
    )(tile_expert, src3, src3, dst3, dst3, h2, wg.astype(BF16), wu.astype(BF16), wd.astype(BF16))


def _combine_kernel(x_ref, mod_ref, g_ref, route_ref, y0_ref, y1_ref, o_ref):
    x = x_ref[0]
    gt = mod_ref[0, 5:6, :]
    route = route_ref[0]
    y = route[:, 2:3] * y0_ref[...] + route[:, 3:4] * y1_ref[...]
    o_ref[0] = x + gt * (_rms(y) * g_ref[3:4, :])


def _combine(x, mod, norm_g, route, ys, tm):
    b, s, d = x.shape
    nj = s // tm
    tok = pl.BlockSpec((1, tm, d), lambda i, j: (i, j, 0))
    return pl.pallas_call(
        _combine_kernel,
        out_shape=jax.ShapeDtypeStruct((b, s, d), F32),
        grid=(b, nj),
        in_specs=[
            tok,
            pl.BlockSpec((1,) + mod.shape[1:], lambda i, j: (i, 0, 0)),
            _const_spec(norm_g.shape),
            pl.BlockSpec((1, tm, LANE), lambda i, j: (i, j, 0)),
            pl.BlockSpec((tm, d), lambda i, j: (i * nj + j, 0)),
            pl.BlockSpec((tm, d), lambda i, j: (b * nj + i * nj + j, 0)),
        ],
        out_specs=tok,
        compiler_params=_params(("arbitrary", "arbitrary")),
        name="moe_combine",
    )(x, mod, norm_g, route, ys, ys)


def _route_maps(idx, n_exp, tm):
    flat = idx.reshape(-1)
    n_pairs = flat.shape[0]
    n_tok = n_pairs // TOP_K
    onehot = (flat[:, None] == jnp.arange(n_exp, dtype=jnp.int32)[None, :]).astype(jnp.int32)
    csum = jnp.cumsum(onehot, axis=0)
    rank = jnp.sum(onehot * csum, axis=1) - 1
    counts = csum[-1]
    tiles = (counts + tm - 1) // tm
    tile_end = jnp.cumsum(tiles)
    tile_start = tile_end - tiles
    pos = tile_start[flat] * tm + rank
    n_tiles = n_pairs // tm + n_exp
    tile_ids = jnp.arange(n_tiles, dtype=jnp.int32)
    tile_expert = jnp.minimum(jnp.sum((tile_end[None, :] <= tile_ids[:, None]).astype(jnp.int32), axis=1),
                              n_exp - 1).astype(jnp.int32)
    n_rows = n_tiles * tm
    pair = jnp.full((n_rows,), -1, jnp.int32).at[pos].set(jnp.arange(n_pairs, dtype=jnp.int32),
                                                         unique_indices=True)
    valid = pair >= 0
    tok, slot = pair // TOP_K, pair % TOP_K
    src = jnp.where(valid, tok, 0).astype(jnp.int32)
    spare = n_pairs + jnp.arange(n_rows, dtype=jnp.int32) % tm
    dst = jnp.where(valid, slot * n_tok + tok, spare).astype(jnp.int32)
    return tile_expert, src, dst, n_pairs + tm


def _moe(x, mod, norm_g, w_router, wg, wu, wd, tm, tm_e):
    b, s, d = x.shape
    n_exp = w_router.shape[1]
    h2, route = _router(x, mod, norm_g, w_router, tm)
    idx = route[..., 0:TOP_K].astype(jnp.int32).reshape(b * s, TOP_K)
    tile_expert, src, dst, n_out = _route_maps(idx, n_exp, tm_e)
    ys = _experts(h2.reshape(b * s, d), tile_expert, src, dst, n_out, wg, wu, wd, tm_e)
    return _combine(x, mod, norm_g, route, ys, tm)


def _tile(s, want):
    t = min(want, s)
    assert s % t == 0 and t % CHUNK == 0
    return t


def kernel(x, c, ada_w, ada_b, norm_g, mix_w_in, conv_w, pool_w, pool_scale, mix_w_out, ffn_w_gate, ffn_w_up, ffn_w_down, rwkv_mu, rwkv_w_r, rwkv_w_k, rwkv_w_v, rwkv_w_o, rwkv_w0, rwkv_w1, rwkv_w2, rwkv_a0, rwkv_a1, rwkv_a2, rwkv_g1, rwkv_g2, rwkv_k_k, rwkv_k_a, rwkv_r_k, rwkv_ln_g, rwkv_ln_b, moe_router, moe_w_gate, moe_w_up, moe_w_down):
    depth = ada_w.shape[0]
    s = x.shape[1]
    tm = _tile(s, 512)
    mods = _ada(c, ada_w, ada_b)
    for layer in range(depth):
        mod, ng, i = mods[layer], norm_g[layer], layer // 2
        if layer % 2 == 0:
            x = _mixer(x, mod, ng, mix_w_in[i], conv_w[i], pool_w[i], pool_scale[i], mix_w_out[i], tm)
            x = _ffn(x, mod, ng, ffn_w_gate[i], ffn_w_up[i], ffn_w_down[i], tm)
        else:
            r, k, v, kk, a, ld, gate = _rwkv_proj(
                x, mod, ng, rwkv_mu[i], rwkv_w_r[i], rwkv_w_k[i], rwkv_w_v[i], rwkv_w0[i], rwkv_w1[i], rwkv_w2[i],
                rwkv_a0[i], rwkv_a1[i], rwkv_a2[i], rwkv_g1[i], rwkv_g2[i], rwkv_k_k[i], rwkv_k_a[i], tm)
            y, bonus = _rwkv_scan(r, k, v, kk, a, ld, rwkv_r_k[i])
            x = _rwkv_out(x, mod, ng, y, bonus, gate, rwkv_ln_g[i], rwkv_ln_b[i], rwkv_w_o[i], tm)
            x = _moe(x, mod, ng, moe_router[i], moe_w_gate[i], moe_w_up[i], moe_w_down[i], tm, tm)
    return x
```

```python
import functools

import jax
import jax.numpy as jnp
from jax import lax
from jax.experimental import pallas as pl
from jax.experimental.pallas import tpu as pltpu

F32 = jnp.float32
BF16 = jnp.bfloat16

RMS_EPS = 1e-6
DECAY_SCALE = 0.6065306597126334
HEAD = 64
QUAD = 4 * HEAD
CHUNK = 64
POOL_WINDOWS = (2, 4, 8, 16)
MAX_WIN = 16
CONV_HALO = 8
LANE = 128
TOP_K = 2
VMEM_LIMIT = 56 * 1024 * 1024


def _dot(a, b):
    return jnp.dot(a, b, preferred_element_type=F32)


def _dot_nt(a, b):
    return lax.dot_general(a, b, (((1,), (1,)), ((), ())), preferred_element_type=F32)


def _dot_f32(a, b):
    return jnp.dot(a, b, preferred_element_type=F32, precision=lax.Precision.HIGHEST)


def _rms(x):
    return x * lax.rsqrt(jnp.mean(x * x, axis=-1, keepdims=True) + RMS_EPS)


def _params(sem):
    return pltpu.CompilerParams(dimension_semantics=sem, vmem_limit_bytes=VMEM_LIMIT)


def _const_spec(shape):
    nd = len(shape)
    return pl.BlockSpec(shape, lambda *_: (0,) * nd, pipeline_mode=pl.Buffered(1))


def _ada_kernel(c_ref, w_ref, b_ref, o_ref):
    c = c_ref[...]
    cond = c * jax.nn.sigmoid(c)
    o_ref[0] = _dot_f32(cond, w_ref[0]) + b_ref[0]


def _ada(c, ada_w, ada_b):
    depth, d, d6 = ada_w.shape
    b = c.shape[0]
    nblk = d6 // d
    out = pl.pallas_call(
        _ada_kernel,
        out_shape=jax.ShapeDtypeStruct((depth, b, d6), F32),
        grid=(depth, nblk),
        in_specs=[
            pl.BlockSpec((b, d), lambda l, n: (0, 0)),
            pl.BlockSpec((1, d, d), lambda l, n: (l, 0, n)),
            pl.BlockSpec((1, 1, d), lambda l, n: (l, 0, n)),
        ],
        out_specs=pl.BlockSpec((1, b, d), lambda l, n: (l, 0, n)),
        compiler_params=_params(("arbitrary", "arbitrary")),
        name="ada_mod",
    )(c, ada_w, ada_b.reshape(depth, 1, d6))
    return out.reshape(depth, b, nblk, d)


def _mix_kernel(x_ref, mod_ref, g_ref, w_in_ref, conv_w_ref, pool_w_ref, pool_s_ref, w_out_ref,
                o_ref, u_halo, v_halo, *, tm, dc, gc):
    j = pl.program_id(1)

    @pl.when(j == 0)
    def _():
        u_halo[...] = jnp.zeros(u_halo.shape, F32)
        v_halo[...] = jnp.zeros(v_halo.shape, F32)

    x = x_ref[0]
    sh, sc, gt = mod_ref[0, 0:1, :], mod_ref[0, 1:2, :], mod_ref[0, 2:3, :]
    h = (_rms(x) * g_ref[0:1, :]) * (1.0 + sc) + sh
    z = _dot(h.astype(BF16), w_in_ref[...])
    b_gate, c_gate, v_conv, v_pool = z[:, 0:dc], z[:, dc:2 * dc], z[:, 2 * dc:3 * dc], z[:, 3 * dc:]

    u = c_gate * v_conv
    u_ext = jnp.concatenate([u_halo[...], u], axis=0)
    um1 = pltpu.roll(u_ext, 1, 0)[CONV_HALO:]
    um2 = pltpu.roll(u_ext, 2, 0)[CONV_HALO:]
    y_conv = b_gate * (um2 * conv_w_ref[0:1, :] + um1 * conv_w_ref[1:2, :] + u * conv_w_ref[2:3, :])
    u_halo[...] = u[tm - CONV_HALO:tm, :]

    sums = jnp.concatenate([v_halo[...], v_pool], axis=0)
    pos = j * tm + lax.broadcasted_iota(jnp.int32, (tm, 1), 0)
    pieces = [y_conv.astype(BF16)]
    width = 1
    for gi, win in enumerate(POOL_WINDOWS):
        assert win == 2 * width
        lo, hi = gi * gc, (gi + 1) * gc
        sums = sums + pltpu.roll(sums, width, 0)
        width = win
        cnt = jnp.minimum(pos + 1, win).astype(F32)
        p = sums[MAX_WIN:, 0:gc] / cnt - v_pool[:, lo:hi]
        sums = sums[:, gc:]
        yp = _dot(p.astype(BF16), pool_w_ref[gi]) * pool_s_ref[0:1, lo:hi]
        pieces.append(yp.astype(BF16))
    v_halo[...] = v_pool[tm - MAX_WIN:tm, :]

    y = _dot(jnp.concatenate(pieces, axis=-1), w_out_ref[...])
    o_ref[0] = x + gt * (_rms(y) * g_ref[1:2, :])


def _mixer(x, mod, norm_g, w_in, conv_w, pool_w, pool_scale, w_out, tm):
    b, s, d = x.shape
    dc = conv_w.shape[1]
    ng, gc = pool_w.shape[0], pool_w.shape[1]
    kern = functools.partial(_mix_kernel, tm=tm, dc=dc, gc=gc)
    return pl.pallas_call(
        kern,
        out_shape=jax.ShapeDtypeStruct((b, s, d), F32),
        grid=(b, s // tm),
        in_specs=[
            pl.BlockSpec((1, tm, d), lambda i, j: (i, j, 0)),
            pl.BlockSpec((1,) + mod.shape[1:], lambda i, j: (i, 0, 0)),
            _const_spec(norm_g.shape),
            _const_spec(w_in.shape),
            _const_spec(conv_w.shape),
            _const_spec(pool_w.shape),
            _const_spec((1, ng * gc)),
            _const_spec(w_out.shape),
        ],
        out_specs=pl.BlockSpec((1, tm, d), lambda i, j: (i, j, 0)),
        scratch_shapes=[pltpu.VMEM((CONV_HALO, dc), F32), pltpu.VMEM((MAX_WIN, ng * gc), F32)],
        compiler_params=_params(("arbitrary", "arbitrary")),
        name="conv_pool_mixer",
    )(x, mod, norm_g, w_in.astype(BF16), conv_w, pool_w.astype(BF16), pool_scale.reshape(1, -1),
      w_out.astype(BF16))


def _ffn_kernel(x_ref, mod_ref, g_ref, wg_ref, wu_ref, wd_ref, o_ref):
    x = x_ref[0]
    sh, sc, gt = mod_ref[0, 3:4, :], mod_ref[0, 4:5, :], mod_ref[0, 5:6, :]
    h = ((_rms(x) * g_ref[2:3, :]) * (1.0 + sc) + sh).astype(BF16)
    g = _dot(h, wg_ref[...])
    u = _dot(h, wu_ref[...])
    a = (g * jax.nn.sigmoid(g) * u).astype(BF16)
    y = _dot(a, wd_ref[...])
    o_ref[0] = x + gt * (_rms(y) * g_ref[3:4, :])


def _ffn(x, mod, norm_g, wg, wu, wd, tm):
    b, s, d = x.shape
    return pl.pallas_call(
        _ffn_kernel,
        out_shape=jax.ShapeDtypeStruct((b, s, d), F32),
        grid=(b, s // tm),
        in_specs=[
            pl.BlockSpec((1, tm, d), lambda i, j: (i, j, 0)),
            pl.BlockSpec((1,) + mod.shape[1:], lambda i, j: (i, 0, 0)),
            _const_spec(norm_g.shape),
            _const_spec(wg.shape),
            _const_spec(wu.shape),
            _const_spec(wd.shape),
        ],
        out_specs=pl.BlockSpec((1, tm, d), lambda i, j: (i, j, 0)),
        compiler_params=_params(("arbitrary", "arbitrary")),
        name="dense_swiglu",
    )(x, mod, norm_g, wg.astype(BF16), wu.astype(BF16), wd.astype(BF16))


def _group_sum(x, ones_bd):
    return _dot(x.astype(BF16), ones_bd)


def _rwkv_proj_kernel(x_ref, mod_ref, g_ref, mu_ref, wr_ref, wk_ref, wv_ref, w1_ref, w2_ref, a1_ref, a2_ref,
                      g1_ref, g2_ref, vec_ref, ones_ref,
                      r_out, k_out, v_out, kk_out, a_out, ld_out, g_out, h_halo, *, tm):
    j = pl.program_id(1)

    @pl.when(j == 0)
    def _():
        h_halo[...] = jnp.zeros(h_halo.shape, F32)

    x = x_ref[0]
    sh, sc = mod_ref[0, 0:1, :], mod_ref[0, 1:2, :]
    h = (_rms(x) * g_ref[0:1, :]) * (1.0 + sc) + sh
    h_prev = pltpu.roll(jnp.concatenate([h_halo[...], h], axis=0), 1, 0)[CONV_HALO:]
    h_halo[...] = h[tm - CONV_HALO:tm, :]
    h_bf = h.astype(BF16)
    xx_bf = (h_prev - h).astype(BF16)

    def mixed(i):
        return h_bf + xx_bf * mu_ref[i:i + 1, :].astype(BF16)

    w0, a0, k_k, k_a = vec_ref[0:1, :], vec_ref[1:2, :], vec_ref[2:3, :], vec_ref[3:4, :]
    r = _dot(mixed(0), wr_ref[...])
    w_lora = _dot(jnp.tanh(_dot(mixed(1), w1_ref[...])).astype(BF16), w2_ref[...])
    k = _dot(mixed(2), wk_ref[...])
    v = _dot(mixed(3), wv_ref[...])
    a_lora = _dot(_dot(mixed(4), a1_ref[...]).astype(BF16), a2_ref[...])
    gate = _dot(jax.nn.sigmoid(_dot(mixed(5), g1_ref[...])).astype(BF16), g2_ref[...])

    log_decay = -DECAY_SCALE * jax.nn.sigmoid(w0 + w_lora)
    a = jax.nn.sigmoid(a0 + a_lora)
    kk = k * k_k
    nrm = jnp.sqrt(_group_sum(kk * kk, ones_ref[...]))
    kk = kk / jnp.maximum(nrm, 1e-12)

    r_out[0] = r
    k_out[0] = k * (1.0 + (a - 1.0) * k_a)
    v_out[0] = v
    kk_out[0] = kk
    a_out[0] = a
    ld_out[0] = log_decay
    g_out[0] = gate


def _pad_cols(w, n):
    return jnp.pad(w, ((0, 0), (0, n - w.shape[1])))


def _pad_rows(w, n):
    return jnp.pad(w, ((0, n - w.shape[0]), (0, 0)))


def _ceil_to(n, m):
    return -(-n // m) * m


def _head_ones(d):
    idx = jnp.arange(d) // HEAD
    return (idx[:, None] == idx[None, :]).astype(BF16)


def _rwkv_proj(x, mod, norm_g, mu, w_r, w_k, w_v, w0, w1, w2, a0, a1, a2, g1, g2, k_k, k_a, tm):
    b, s, d = x.shape
    lw, la, lg = (_ceil_to(w.shape[1], LANE) for w in (w1, a1, g1))
    w1p, w2p = _pad_cols(w1, lw).astype(BF16), _pad_rows(w2, lw).astype(BF16)
    a1p, a2p = _pad_cols(a1, la).astype(BF16), _pad_rows(a2, la).astype(BF16)
    g1p, g2p = _pad_cols(g1, lg).astype(BF16), _pad_rows(g2, lg).astype(BF16)
    vecs = jnp.stack([w0, a0, k_k, k_a], axis=0)
    consts = [norm_g, mu, w_r.astype(BF16), w_k.astype(BF16), w_v.astype(BF16), w1p, w2p, a1p, a2p, g1p, g2p,
              vecs, _head_ones(d)]
    tok = pl.BlockSpec((1, tm, d), lambda i, j: (i, j, 0))
    outs = pl.pallas_call(
        functools.partial(_rwkv_proj_kernel, tm=tm),
        out_shape=[jax.ShapeDtypeStruct((b, s, d), F32)] * 7,
        grid=(b, s // tm),
        in_specs=[tok, pl.BlockSpec((1,) + mod.shape[1:], lambda i, j: (i, 0, 0))]
                 + [_const_spec(c.shape) for c in consts],
        out_specs=[tok] * 7,
        scratch_shapes=[pltpu.VMEM((CONV_HALO, d), F32)],
        compiler_params=_params(("arbitrary", "arbitrary")),
        name="rwkv_proj",
    )(x, mod, *consts)
    return outs


PAIR = 2


def _rwkv_scan_kernel(r_ref, k_ref, v_ref, kk_ref, a_ref, ld_ref, rk_ref, y_ref, bonus_ref,
                      ht_s, lhs1_s, arb_s, ark_s, uv_s, v_s, bk_s, gl_s, *, nquad):
    step = pl.program_id(1)

    @pl.when(step == 0)
    def _():
        for ref in (ht_s, lhs1_s, arb_s, ark_s, uv_s, v_s, bk_s, gl_s):
            ref[...] = jnp.zeros(ref.shape, ref.dtype)

    row = lax.broadcasted_iota(jnp.int32, (QUAD, QUAD), 0) // HEAD
    col = lax.broadcasted_iota(jnp.int32, (QUAD, QUAD), 1) // HEAD
    blockmask = row == col
    t_i = lax.broadcasted_iota(jnp.int32, (CHUNK, QUAD), 0)
    s_i = lax.broadcasted_iota(jnp.int32, (CHUNK, QUAD), 1) % HEAD
    strict = s_i < t_i
    incl = s_i <= t_i
    blk16 = (s_i // 16) == (t_i // 16)
    off1 = ((s_i // 16) + 1 == (t_i // 16)) & ((t_i // 16) % 2 == 1)
    off2 = (s_i // 32 == 0) & (t_i // 32 == 1)
    eye_q = (s_i == t_i).astype(F32)
    ltri = (lax.broadcasted_iota(jnp.int32, (CHUNK, CHUNK), 1)
            <= lax.broadcasted_iota(jnp.int32, (CHUNK, CHUNK), 0)).astype(BF16)

    def bd(x):
        x4 = jnp.concatenate([x, x, x, x], axis=0)
        return jnp.where(blockmask, x4, 0.0).astype(BF16)

    def mm(a, b_bd):
        return _dot(a.astype(BF16), b_bd)

    def stack(a, b):
        return jnp.concatenate([a, b], axis=0).astype(BF16)

    probs = [(c, q) for c in range(PAIR) for q in range(nquad)]
    P = {}
    hts = [ht_s[q] for q in range(nquad)]
    S = {}

    def p1_prep():
        for c in range(PAIR):
            rows = slice(c * CHUNK, (c + 1) * CHUNK)
            ld_c = ld_ref[0, rows, :]
            hi = ld_c.astype(BF16)
            rem = ld_c - hi.astype(F32)
            mid = rem.astype(BF16)
            lo = (rem - mid.astype(F32)).astype(BF16)
            cum_c = _dot(ltri, hi) + _dot(ltri, mid) + _dot(ltri, lo)
            for q in range(nquad):
                sl = slice(q * QUAD, (q + 1) * QUAD)
                rc, kc, vc, kkc, ac = (ref[0, rows, sl] for ref in (r_ref, k_ref, v_ref, kk_ref, a_ref))
                cum, ldc = cum_c[:, sl], ld_c[:, sl]
                last = cum[CHUNK - 1:CHUNK, :]
                g_inv = jnp.exp(-cum)
                g_rem = jnp.exp(last - cum)
                bsrc = kkc * ac
                at = -kkc * jnp.exp(cum - ldc)
                rt = rc * jnp.exp(cum)
                P[c, q] = dict(at=at, rt=rt, v=vc, bt_bd=bd(bsrc * g_inv), kt_bd=bd(kc * g_inv),
                               rtk=(rt * rk_ref[0:1, sl]).astype(BF16),
                               bk=stack(bsrc * g_rem, kc * g_rem), gl=jnp.exp(last))

    def p1_amat():
        for p in probs:
            d = P[p]
            lhs = stack(d["at"], d["rt"])
            ab = _dot_nt(lhs, d.pop("bt_bd"))
            ak = _dot_nt(jnp.concatenate([lhs, d.pop("rtk")], axis=0), d.pop("kt_bd"))
            d["a_ab"] = jnp.where(strict, ab[0:CHUNK], 0.0)
            d["a_rb"] = jnp.where(incl, ab[CHUNK:], 0.0)
            d["a_ak"] = jnp.where(strict, ak[0:CHUNK], 0.0)
            d["a_rk"] = jnp.where(incl, ak[CHUNK:2 * CHUNK], 0.0)
            d["bonus"] = jnp.where(s_i == t_i, ak[2 * CHUNK:], 0.0)

    def p1_n1():
        for p in probs:
            d = P[p]
            ad = jnp.where(blk16, d["a_ab"], 0.0)
            d["p"] = eye_q + ad
            d["pw"] = mm(ad, bd(ad))

    def p1_n2():
        for p in probs:
            d = P[p]
            both = _dot(stack(d["p"], d["pw"]), bd(d["pw"]))
            d["p"] = d["p"] + both[0:CHUNK]
            d["pw"] = both[CHUNK:]

    def p1_n4():
        for p in probs:
            d = P[p]
            d["p"] = d["p"] + mm(d["p"], bd(d.pop("pw")))

    def p1_m1(mask):
        def run():
            for p in probs:
                d = P[p]
                d["p_bd"] = bd(d["p"])
                d["t"] = mm(d["p"], bd(jnp.where(mask, d["a_ab"], 0.0)))
        return run

    def p1_m2():
        for p in probs:
            d = P[p]
            d["p"] = d["p"] + mm(d.pop("t"), d.pop("p_bd"))

    def p1_akv():
        for (c, q) in probs:
            d = P[c, q]
            both = _dot(stack(d.pop("a_ak"), d.pop("bonus")), bd(d["v"]))
            d["akv"] = both[0:CHUNK]
            bonus_ref[0, c * CHUNK:(c + 1) * CHUNK, q * QUAD:(q + 1) * QUAD] = both[CHUNK:]

    def p1_solve():
        for p in probs:
            d = P[p]
            w = d.pop("p").astype(BF16)
            d["wa"] = _dot(w, bd(d.pop("at")))
            d["uv"] = _dot(w, bd(d.pop("akv")))

    wr_base = (step % 2) * len(probs)
    rd_base = len(probs) - wr_base

    def p1_store():
        for n, p in enumerate(probs):
            d, i = P[p], wr_base + n
            lhs1_s[i] = stack(d["wa"], d["rt"])
            arb_s[i] = d["a_rb"].astype(BF16)
            ark_s[i] = d["a_rk"].astype(BF16)
            uv_s[i] = d["uv"]
            v_s[i] = d["v"]
            bk_s[i] = d["bk"]
            gl_s[i] = jnp.broadcast_to(d["gl"], (8, QUAD))

    def p2_a(c):
        def run():
            for q in range(nquad):
                i = rd_base + c * nquad + q
                uy = _dot_nt(lhs1_s[i], hts[q].astype(BF16))
                S[q] = dict(u=uy[0:CHUNK] + uv_s[i], y1=uy[CHUNK:])
        return run

    def p2_b(c):
        def run():
            rows = slice(c * CHUNK, (c + 1) * CHUNK)
            for q in range(nquad):
                i = rd_base + c * nquad + q
                d = S.pop(q)
                u, vc = d["u"], v_s[i]
                y_ref[0, rows, q * QUAD:(q + 1) * QUAD] = d["y1"] + _dot(arb_s[i], bd(u)) + _dot(ark_s[i], bd(vc))
                uv_t = jnp.concatenate([u, vc], axis=0).T.astype(BF16)
                upd = _dot(uv_t, bk_s[i])
                hts[q] = gl_s[i][0:1, :] * hts[q] + jnp.where(blockmask, upd, 0.0)
        return run

    def p2_store():
        for q in range(nquad):
            ht_s[q] = hts[q]

    for stage in (p1_prep, p2_a(0), p1_amat, p2_b(0), p1_n1, p2_a(1), p1_n2, p2_b(1), p2_store, p1_n2, p1_n4,
                  p1_m1(off1), p1_m2, p1_m1(off2), p1_m2, p1_akv, p1_solve, p1_store):
        stage()


def _rwkv_scan(r, k, v, kk, a, ld, r_k):
    b, s, d = r.shape
    nquad = d // QUAD
    span = PAIR * CHUNK
    npair = s // span
    nprob = 2 * PAIR * nquad
    blk_in = pl.BlockSpec((1, span, d), lambda i, c: (i, jnp.minimum(c, npair - 1), 0))
    blk_out = pl.BlockSpec((1, span, d), lambda i, c: (i, jnp.maximum(c - 1, 0), 0))
    return pl.pallas_call(
        functools.partial(_rwkv_scan_kernel, nquad=nquad),
        out_shape=[jax.ShapeDtypeStruct((b, s, d), F32)] * 2,
        grid=(b, npair + 1),
        in_specs=[blk_in] * 6 + [_const_spec((1, d))],
        out_specs=[blk_out, blk_in],
        scratch_shapes=[
            pltpu.VMEM((nquad, QUAD, QUAD), F32),
            pltpu.VMEM((nprob, 2 * CHUNK, QUAD), BF16),
            pltpu.VMEM((nprob, CHUNK, QUAD), BF16),
            pltpu.VMEM((nprob, CHUNK, QUAD), BF16),
            pltpu.VMEM((nprob, CHUNK, QUAD), F32),
            pltpu.VMEM((nprob, CHUNK, QUAD), F32),
            pltpu.VMEM((nprob, 2 * CHUNK, QUAD), BF16),
            pltpu.VMEM((nprob, 8, QUAD), F32),
        ],
        compiler_params=_params(("arbitrary", "arbitrary")),
        name="rwkv_scan",
    )(r, k, v, kk, a, ld, r_k.reshape(1, d))


def _rwkv_out_kernel(x_ref, mod_ref, g_ref, y_ref, bonus_ref, gate_ref, vec_ref, ones_ref, wo_ref,
                     o_ref, *, ln_eps):
    x = x_ref[0]
    gt = mod_ref[0, 2:3, :]
    ln_g, ln_b = vec_ref[0:1, :], vec_ref[1:2, :]
    ones_bd = ones_ref[...]
    y = y_ref[0]
    mean = _group_sum(y, ones_bd) * (1.0 / HEAD)
    dlt = y - mean
    var = _group_sum(dlt * dlt, ones_bd) * (1.0 / HEAD)
    yn = dlt * lax.rsqrt(var + ln_eps) * ln_g + ln_b
    out = _dot(((yn + bonus_ref[0]) * gate_ref[0]).astype(BF16), wo_ref[...])
    o_ref[0] = x + gt * (_rms(out) * g_ref[1:2, :])


def _rwkv_out(x, mod, norm_g, y, bonus, gate, ln_g, ln_b, w_o, tm):
    b, s, d = x.shape
    vecs = jnp.stack([ln_g, ln_b], axis=0)
    tok = pl.BlockSpec((1, tm, d), lambda i, j: (i, j, 0))
    consts = [vecs, _head_ones(d), w_o.astype(BF16)]
    return pl.pallas_call(
        functools.partial(_rwkv_out_kernel, ln_eps=1e-5 * HEAD),
        out_shape=jax.ShapeDtypeStruct((b, s, d), F32),
        grid=(b, s // tm),
        in_specs=[tok, pl.BlockSpec((1,) + mod.shape[1:], lambda i, j: (i, 0, 0)), _const_spec(norm_g.shape)]
                 + [tok] * 3 + [_const_spec(c.shape) for c in consts],
        out_specs=tok,
        compiler_params=_params(("arbitrary", "arbitrary")),
        name="rwkv_out",
    )(x, mod, norm_g, y, bonus, gate, *consts)


def _router_kernel(x_ref, mod_ref, g_ref, wr_ref, h_out, route_out, *, n_exp):
    x = x_ref[0]
    sh, sc = mod_ref[0, 3:4, :], mod_ref[0, 4:5, :]
    h = (_rms(x) * g_ref[2:3, :]) * (1.0 + sc) + sh
    h_out[0] = h
    logits = _dot_f32(h, wr_ref[...])
    lane = lax.broadcasted_iota(jnp.int32, logits.shape, 1)
    neg = jnp.float32(-jnp.inf)
    logits = jnp.where(lane < n_exp, logits, neg)
    m1 = jnp.max(logits, axis=-1, keepdims=True)
    i1 = jnp.min(jnp.where(logits == m1, lane, LANE), axis=-1, keepdims=True)
    rest = jnp.where(lane == i1, neg, logits)
    m2 = jnp.max(rest, axis=-1, keepdims=True)
    i2 = jnp.min(jnp.where(rest == m2, lane, LANE), axis=-1, keepdims=True)
    w1 = 1.0 / (1.0 + jnp.exp(m2 - m1))
    w2 = 1.0 - w1
    route = jnp.where(lane == 0, i1.astype(F32), jnp.where(lane == 1, i2.astype(F32),
                      jnp.where(lane == 2, w1, jnp.where(lane == 3, w2, 0.0))))
    route_out[0] = route


def _router(x, mod, norm_g, w_router, tm):
    b, s, d = x.shape
    n_exp = w_router.shape[1]
    wr = _pad_cols(w_router, LANE)
    tok = pl.BlockSpec((1, tm, d), lambda i, j: (i, j, 0))
    return pl.pallas_call(
        functools.partial(_router_kernel, n_exp=n_exp),
        out_shape=[jax.ShapeDtypeStruct((b, s, d), F32), jax.ShapeDtypeStruct((b, s, LANE), F32)],
        grid=(b, s // tm),
        in_specs=[tok, pl.BlockSpec((1,) + mod.shape[1:], lambda i, j: (i, 0, 0)), _const_spec(norm_g.shape),
                  _const_spec(wr.shape)],
        out_specs=[tok, pl.BlockSpec((1, tm, LANE), lambda i, j: (i, j, 0))],
        compiler_params=_params(("arbitrary", "arbitrary")),
        name="moe_router",
    )(x, mod, norm_g, wr)


def _row_copy(src, src_row, dst, dst_row, sem):
    return pltpu.make_async_copy(src.at[pl.ds(src_row, 1)], dst.at[pl.ds(dst_row, 1)], sem)


def _expert_kernel(te_ref, src_cur, src_next, dst_prev, dst_cur, h_ref, wg_ref, wu_ref, wd_ref, out_ref,
                   xbuf, ybuf, gsem, ssem, *, tm):
    del te_ref
    i = pl.program_id(0)
    last = pl.num_programs(0) - 1
    cur, nxt = i % 2, (i + 1) % 2

    def rows_of(idx_ref):
        rows, z = [], 0
        for r in range(tm):
            row = idx_ref[0, 0, r + z]
            z = lax.shift_right_arithmetic(row, 31)
            rows.append(row)
        return rows

    def gather(idx_ref, slot):
        return [(_row_copy(h_ref, row, xbuf.at[slot], r, gsem), 0) for r, row in enumerate(rows_of(idx_ref))]

    def scatter(idx_ref, slot):
        return [(_row_copy(ybuf.at[slot], r, out_ref, row, ssem), 1) for r, row in enumerate(rows_of(idx_ref))]

    def start(copies):
        for cp, dma_queue in copies:
            cp.start(priority=dma_queue)

    def run(copies):
        start(copies)
        for cp, _ in copies:
            cp.wait()

    @pl.when(i == 0)
    def _():
        ybuf[...] = jnp.zeros(ybuf.shape, F32)
        run(gather(src_cur, 0))

    copies = gather(src_next, nxt) + scatter(dst_prev, nxt)
    start(copies)
    h = xbuf[cur].astype(BF16)
    g = _dot(h, wg_ref[0])
    u = _dot(h, wu_ref[0])
    a = (g * jax.nn.sigmoid(g) * u).astype(BF16)
    ybuf[cur] = _dot(a, wd_ref[0])
    for cp, _ in copies:
        cp.wait()

    @pl.when(i == last)
    def _():
        run(scatter(dst_cur, cur))


def _experts(h2, tile_expert, src, dst, n_out, wg, wu, wd, tm):
    t, d = h2.shape
    f = wg.shape[2]
    n_tiles = src.shape[0] // tm
    src3, dst3 = src.reshape(n_tiles, 1, tm), dst.reshape(n_tiles, 1, tm)

    def idx_spec(fn):
        return pl.BlockSpec((1, 1, tm), lambda i, te: (fn(i), 0, 0), memory_space=pltpu.SMEM)

    def w_spec(shape):
        return pl.BlockSpec((1,) + shape, lambda i, te: (te[i], 0, 0), pipeline_mode=pl.Buffered(1))

    grid_spec = pltpu.PrefetchScalarGridSpec(
        num_scalar_prefetch=1,
        grid=(n_tiles,),
        in_specs=[
            idx_spec(lambda i: i),
            idx_spec(lambda i: jnp.minimum(i + 1, n_tiles - 1)),
            idx_spec(lambda i: jnp.maximum(i - 1, 0)),
            idx_spec(lambda i: i),
            pl.BlockSpec(memory_space=pl.ANY),
            w_spec((d, f)), w_spec((d, f)), w_spec((f, d)),
        ],
        out_specs=pl.BlockSpec(memory_space=pl.ANY),
        scratch_shapes=[pltpu.VMEM((2, tm, d), F32), pltpu.VMEM((2, tm, d), F32),
                        pltpu.SemaphoreType.DMA(()), pltpu.SemaphoreType.DMA(())],
    )
    return pl.pallas_call(
        functools.partial(_expert_kernel, tm=tm),
        out_shape=jax.ShapeDtypeStruct((n_out, d), F32),
        grid_spec=grid_spec,
        compiler_params=_params(("arbitrary",)),
        name="moe_experts",
    )(tile_expert, src3, src3, dst3, dst3, h2, wg.astype(BF16), wu.astype(BF16), wd.astype(BF16))


def _combine_kernel(x_ref, mod_ref, g_ref, route_ref, y0_ref, y1_ref, o_ref):
    x = x_ref[0]
    gt = mod_ref[0, 5:6, :]
    route = route_ref[0]
    y = route[:, 2:3] * y0_ref[...] + route[:, 3:4] * y1_ref[...]
    o_ref[0] = x + gt * (_rms(y) * g_ref[3:4, :])


def _combine(x, mod, norm_g, route, ys, tm):
    b, s, d = x.shape
    nj = s // tm
    tok = pl.BlockSpec((1, tm, d), lambda i, j: (i, j, 0))
    return pl.pallas_call(
        _combine_kernel,
        out_shape=jax.ShapeDtypeStruct((b, s, d), F32),
        grid=(b, nj),
        in_specs=[
            tok,
            pl.BlockSpec((1,) + mod.shape[1:], lambda i, j: (i, 0, 0)),
            _const_spec(norm_g.shape),
            pl.BlockSpec((1, tm, LANE), lambda i, j: (i, j, 0)),
            pl.BlockSpec((tm, d), lambda i, j: (i * nj + j, 0)),
            pl.BlockSpec((tm, d), lambda i, j: (b * nj + i * nj + j, 0)),
        ],
        out_specs=tok,
        compiler_params=_params(("arbitrary", "arbitrary")),
        name="moe_combine",
    )(x, mod, norm_g, route, ys, ys)


def _route_maps(idx, n_exp, tm):
    flat = idx.reshape(-1)
    n_pairs = flat.shape[0]
    n_tok = n_pairs // TOP_K
    onehot = (flat[:, None] == jnp.arange(n_exp, dtype=jnp.int32)[None, :]).astype(jnp.int32)
    csum = jnp.cumsum(onehot, axis=0)
    rank = jnp.sum(onehot * csum, axis=1) - 1
    counts = csum[-1]
    tiles = (counts + tm - 1) // tm
    tile_end = jnp.cumsum(tiles)
    tile_start = tile_end - tiles
    pos = tile_start[flat] * tm + rank
    n_tiles = n_pairs // tm + n_exp
    tile_ids = jnp.arange(n_tiles, dtype=jnp.int32)
    tile_expert = jnp.minimum(jnp.sum((tile_end[None, :] <= tile_ids[:, None]).astype(jnp.int32), axis=1),
                              n_exp - 1).astype(jnp.int32)
    n_rows = n_tiles * tm
    pair = jnp.full((n_rows,), -1, jnp.int32).at[pos].set(jnp.arange(n_pairs, dtype=jnp.int32),
                                                         unique_indices=True)
    valid = pair >= 0
    tok, slot = pair // TOP_K, pair % TOP_K
    src = jnp.where(valid, tok, 0).astype(jnp.int32)
    spare = n_pairs + jnp.arange(n_rows, dtype=jnp.int32) % tm
    dst = jnp.where(valid, slot * n_tok + tok, spare).astype(jnp.int32)
    return tile_expert, src, dst, n_pairs + tm


def _moe(x, mod, norm_g, w_router, wg, wu, wd, tm, tm_e):
    b, s, d = x.shape
    n_exp = w_router.shape[1]
    h2, route = _router(x, mod, norm_g, w_router, tm)
    idx = route[..., 0:TOP_K].astype(jnp.int32).reshape(b * s, TOP_K)
    tile_expert, src, dst, n_out = _route_maps(idx, n_exp, tm_e)
    ys = _experts(h2.reshape(b * s, d), tile_expert, src, dst, n_out, wg, wu, wd, tm_e)
    return _combine(x, mod, norm_g, route, ys, tm)


def _tile(s, want):
    t = min(want, s)
    assert s % t == 0 and t % CHUNK == 0
    return t


def kernel(x, c, ada_w, ada_b, norm_g, mix_w_in, conv_w, pool_w, pool_scale, mix_w_out, ffn_w_gate, ffn_w_up, ffn_w_down, rwkv_mu, rwkv_w_r, rwkv_w_k, rwkv_w_v, rwkv_w_o, rwkv_w0, rwkv_w1, rwkv_w2, rwkv_a0, rwkv_a1, rwkv_a2, rwkv_g1, rwkv_g2, rwkv_k_k, rwkv_k_a, rwkv_r_k, rwkv_ln_g, rwkv_ln_b, moe_router, moe_w_gate, moe_w_up, moe_w_down):
    depth = ada_w.shape[0]
    s = x.shape[1]
    tm = _tile(s, 512)
    mods = _ada(c, ada_w, ada_b)
    for layer in range(depth):
        mod, ng, i = mods[layer], norm_g[layer], layer // 2
        if layer % 2 == 0:
            x = _mixer(x, mod, ng, mix_w_in[i], conv_w[i], pool_w[i], pool_scale[i], mix_w_out[i], tm)
            x = _ffn(x, mod, ng, ffn_w_gate[i], ffn_w_up[i], ffn_w_down[i], tm)
        else:
            r, k, v, kk, a, ld, gate = _rwkv_proj(
                x, mod, ng, rwkv_mu[i], rwkv_w_r[i], rwkv_w_k[i], rwkv_w_v[i], rwkv_w0[i], rwkv_w1[i], rwkv_w2[i],
                rwkv_a0[i], rwkv_a1[i], rwkv_a2[i], rwkv_g1[i], rwkv_g2[i], rwkv_k_k[i], rwkv_k_a[i], tm)
            y, bonus = _rwkv_scan(r, k, v, kk, a, ld, rwkv_r_k[i])
            x = _rwkv_out(x, mod, ng, y, bonus, gate, rwkv_ln_g[i], rwkv_ln_b[i], rwkv_w_o[i], tm)
            x = _moe(x, mod, ng, moe_router[i], moe_w_gate[i], moe_w_up[i], moe_w_down[i], tm, tm)
    return x
```

```python
import functools

import jax
import jax.numpy as jnp
from jax import lax
from jax.experimental import pallas as pl
from jax.experimental.pallas import tpu as pltpu

F32 = jnp.float32
BF16 = jnp.bfloat16

RMS_EPS = 1e-6
DECAY_SCALE = 0.6065306597126334
HEAD = 64
QUAD = 4 * HEAD
CHUNK = 64
POOL_WINDOWS = (2, 4, 8, 16)
MAX_WIN = 16
CONV_HALO = 8
LANE = 128
TOP_K = 2
VMEM_LIMIT = 56 * 1024 * 1024


def _dot(a, b):
    return jnp.dot(a, b, preferred_element_type=F32)


def _dot_nt(a, b):
    return lax.dot_general(a, b, (((1,), (1,)), ((), ())), preferred_element_type=F32)


def _dot_f32(a, b):
    return jnp.dot(a, b, preferred_element_type=F32, precision=lax.Precision.HIGHEST)


def _rms(x):
    return x * lax.rsqrt(jnp.mean(x * x, axis=-1, keepdims=True) + RMS_EPS)


def _params(sem):
    return pltpu.CompilerParams(dimension_semantics=sem, vmem_limit_bytes=VMEM_LIMIT)


def _const_spec(shape):
    nd = len(shape)
    return pl.BlockSpec(shape, lambda *_: (0,) * nd, pipeline_mode=pl.Buffered(1))


def _ada_kernel(c_ref, w_ref, b_ref, o_ref):
    c = c_ref[...]
    cond = c * jax.nn.sigmoid(c)
    o_ref[0] = _dot_f32(cond, w_ref[0]) + b_ref[0]


def _ada(c, ada_w, ada_b):
    depth, d, d6 = ada_w.shape
    b = c.shape[0]
    nblk = d6 // d
    out = pl.pallas_call(
        _ada_kernel,
        out_shape=jax.ShapeDtypeStruct((depth, b, d6), F32),
        grid=(depth, nblk),
        in_specs=[
            pl.BlockSpec((b, d), lambda l, n: (0, 0)),
            pl.BlockSpec((1, d, d), lambda l, n: (l, 0, n)),
            pl.BlockSpec((1, 1, d), lambda l, n: (l, 0, n)),
        ],
        out_specs=pl.BlockSpec((1, b, d), lambda l, n: (l, 0, n)),
        compiler_params=_params(("arbitrary", "arbitrary")),
        name="ada_mod",
    )(c, ada_w, ada_b.reshape(depth, 1, d6))
    return out.reshape(depth, b, nblk, d)


def _mix_kernel(x_ref, mod_ref, g_ref, w_in_ref, conv_w_ref, pool_w_ref, pool_s_ref, w_out_ref,
                o_ref, u_halo, v_halo, *, tm, dc, gc):
    j = pl.program_id(1)

    @pl.when(j == 0)
    def _():
        u_halo[...] = jnp.zeros(u_halo.shape, F32)
        v_halo[...] = jnp.zeros(v_halo.shape, F32)

    x = x_ref[0]
    sh, sc, gt = mod_ref[0, 0:1, :], mod_ref[0, 1:2, :], mod_ref[0, 2:3, :]
    h = (_rms(x) * g_ref[0:1, :]) * (1.0 + sc) + sh
    z = _dot(h.astype(BF16), w_in_ref[...])
    b_gate, c_gate, v_conv, v_pool = z[:, 0:dc], z[:, dc:2 * dc], z[:, 2 * dc:3 * dc], z[:, 3 * dc:]

    u = c_gate * v_conv
    u_ext = jnp.concatenate([u_halo[...], u], axis=0)
    um1 = pltpu.roll(u_ext, 1, 0)[CONV_HALO:]
    um2 = pltpu.roll(u_ext, 2, 0)[CONV_HALO:]
    y_conv = b_gate * (um2 * conv_w_ref[0:1, :] + um1 * conv_w_ref[1:2, :] + u * conv_w_ref[2:3, :])
    u_halo[...] = u[tm - CONV_HALO:tm, :]

    sums = jnp.concatenate([v_halo[...], v_pool], axis=0)
    pos = j * tm + lax.broadcasted_iota(jnp.int32, (tm, 1), 0)
    pieces = [y_conv.astype(BF16)]
    width = 1
    for gi, win in enumerate(POOL_WINDOWS):
        assert win == 2 * width
        lo, hi = gi * gc, (gi + 1) * gc
        sums = sums + pltpu.roll(sums, width, 0)
        width = win
        cnt = jnp.minimum(pos + 1, win).astype(F32)
        p = sums[MAX_WIN:, 0:gc] / cnt - v_pool[:, lo:hi]
        sums = sums[:, gc:]
        yp = _dot(p.astype(BF16), pool_w_ref[gi]) * pool_s_ref[0:1, lo:hi]
        pieces.append(yp.astype(BF16))
    v_halo[...] = v_pool[tm - MAX_WIN:tm, :]

    y = _dot(jnp.concatenate(pieces, axis=-1), w_out_ref[...])
    o_ref[0] = x + gt * (_rms(y) * g_ref[1:2, :])


def _mixer(x, mod, norm_g, w_in, conv_w, pool_w, pool_scale, w_out, tm):
    b, s, d = x.shape
    dc = conv_w.shape[1]
    ng, gc = pool_w.shape[0], pool_w.shape[1]
    kern = functools.partial(_mix_kernel, tm=tm, dc=dc, gc=gc)
    return pl.pallas_call(
        kern,
        out_shape=jax.ShapeDtypeStruct((b, s, d), F32),
        grid=(b, s // tm),
        in_specs=[
            pl.BlockSpec((1, tm, d), lambda i, j: (i, j, 0)),
            pl.BlockSpec((1,) + mod.shape[1:], lambda i, j: (i, 0, 0)),
            _const_spec(norm_g.shape),
            _const_spec(w_in.shape),
            _const_spec(conv_w.shape),
            _const_spec(pool_w.shape),
            _const_spec((1, ng * gc)),
            _const_spec(w_out.shape),
        ],
        out_specs=pl.BlockSpec((1, tm, d), lambda i, j: (i, j, 0)),
        scratch_shapes=[pltpu.VMEM((CONV_HALO, dc), F32), pltpu.VMEM((MAX_WIN, ng * gc), F32)],
        compiler_params=_params(("arbitrary", "arbitrary")),
        name="conv_pool_mixer",
    )(x, mod, norm_g, w_in.astype(BF16), conv_w, pool_w.astype(BF16), pool_scale.reshape(1, -1),
      w_out.astype(BF16))


def _ffn_kernel(x_ref, mod_ref, g_ref, wg_ref, wu_ref, wd_ref, o_ref):
    x = x_ref[0]
    sh, sc, gt = mod_ref[0, 3:4, :], mod_ref[0, 4:5, :], mod_ref[0, 5:6, :]
    h = ((_rms(x) * g_ref[2:3, :]) * (1.0 + sc) + sh).astype(BF16)
    g = _dot(h, wg_ref[...])
    u = _dot(h, wu_ref[...])
    a = (g * jax.nn.sigmoid(g) * u).astype(BF16)
    y = _dot(a, wd_ref[...])
    o_ref[0] = x + gt * (_rms(y) * g_ref[3:4, :])


def _ffn(x, mod, norm_g, wg, wu, wd, tm):
    b, s, d = x.shape
    return pl.pallas_call(
        _ffn_kernel,
        out_shape=jax.ShapeDtypeStruct((b, s, d), F32),
        grid=(b, s // tm),
        in_specs=[
            pl.BlockSpec((1, tm, d), lambda i, j: (i, j, 0)),
            pl.BlockSpec((1,) + mod.shape[1:], lambda i, j: (i, 0, 0)),
            _const_spec(norm_g.shape),
            _const_spec(wg.shape),
            _const_spec(wu.shape),
            _const_spec(wd.shape),
        ],
        out_specs=pl.BlockSpec((1, tm, d), lambda i, j: (i, j, 0)),
        compiler_params=_params(("arbitrary", "arbitrary")),
        name="dense_swiglu",
    )(x, mod, norm_g, wg.astype(BF16), wu.astype(BF16), wd.astype(BF16))


def _group_sum(x, ones_bd):
    return _dot(x.astype(BF16), ones_bd)


def _rwkv_proj_kernel(x_ref, mod_ref, g_ref, mu_ref, wr_ref, wk_ref, wv_ref, w1_ref, w2_ref, a1_ref, a2_ref,
                      g1_ref, g2_ref, vec_ref, ones_ref,
                      r_out, k_out, v_out, kk_out, a_out, ld_out, g_out, h_halo, *, tm):
    j = pl.program_id(1)

    @pl.when(j == 0)
    def _():
        h_halo[...] = jnp.zeros(h_halo.shape, F32)

    x = x_ref[0]
    sh, sc = mod_ref[0, 0:1, :], mod_ref[0, 1:2, :]
    h = (_rms(x) * g_ref[0:1, :]) * (1.0 + sc) + sh
    h_prev = pltpu.roll(jnp.concatenate([h_halo[...], h], axis=0), 1, 0)[CONV_HALO:]
    h_halo[...] = h[tm - CONV_HALO:tm, :]
    h_bf = h.astype(BF16)
    xx_bf = (h_prev - h).astype(BF16)

    def mixed(i):
        return h_bf + xx_bf * mu_ref[i:i + 1, :].astype(BF16)

    w0, a0, k_k, k_a = vec_ref[0:1, :], vec_ref[1:2, :], vec_ref[2:3, :], vec_ref[3:4, :]
    r = _dot(mixed(0), wr_ref[...])
    w_lora = _dot(jnp.tanh(_dot(mixed(1), w1_ref[...])).astype(BF16), w2_ref[...])
    k = _dot(mixed(2), wk_ref[...])
    v = _dot(mixed(3), wv_ref[...])
    a_lora = _dot(_dot(mixed(4), a1_ref[...]).astype(BF16), a2_ref[...])
    gate = _dot(jax.nn.sigmoid(_dot(mixed(5), g1_ref[...])).astype(BF16), g2_ref[...])

    log_decay = -DECAY_SCALE * jax.nn.sigmoid(w0 + w_lora)
    a = jax.nn.sigmoid(a0 + a_lora)
    kk = k * k_k
    nrm = jnp.sqrt(_group_sum(kk * kk, ones_ref[...]))
    kk = kk / jnp.maximum(nrm, 1e-12)

    r_out[0] = r
    k_out[0] = k * (1.0 + (a - 1.0) * k_a)
    v_out[0] = v
    kk_out[0] = kk
    a_out[0] = a
    ld_out[0] = log_decay
    g_out[0] = gate


def _pad_cols(w, n):
    return jnp.pad(w, ((0, 0), (0, n - w.shape[1])))


def _pad_rows(w, n):
    return jnp.pad(w, ((0, n - w.shape[0]), (0, 0)))


def _ceil_to(n, m):
    return -(-n // m) * m


def _head_ones(d):
    idx = jnp.arange(d) // HEAD
    return (idx[:, None] == idx[None, :]).astype(BF16)


def _rwkv_proj(x, mod, norm_g, mu, w_r, w_k, w_v, w0, w1, w2, a0, a1, a2, g1, g2, k_k, k_a, tm):
    b, s, d = x.shape
    lw, la, lg = (_ceil_to(w.shape[1], LANE) for w in (w1, a1, g1))
    w1p, w2p = _pad_cols(w1, lw).astype(BF16), _pad_rows(w2, lw).astype(BF16)
    a1p, a2p = _pad_cols(a1, la).astype(BF16), _pad_rows(a2, la).astype(BF16)
    g1p, g2p = _pad_cols(g1, lg).astype(BF16), _pad_rows(g2, lg).astype(BF16)
    vecs = jnp.stack([w0, a0, k_k, k_a], axis=0)
    consts = [norm_g, mu, w_r.astype(BF16), w_k.astype(BF16), w_v.astype(BF16), w1p, w2p, a1p, a2p, g1p, g2p,
              vecs, _head_ones(d)]
    tok = pl.BlockSpec((1, tm, d), lambda i, j: (i, j, 0))
    outs = pl.pallas_call(
        functools.partial(_rwkv_proj_kernel, tm=tm),
        out_shape=[jax.ShapeDtypeStruct((b, s, d), F32)] * 7,
        grid=(b, s // tm),
        in_specs=[tok, pl.BlockSpec((1,) + mod.shape[1:], lambda i, j: (i, 0, 0))]
                 + [_const_spec(c.shape) for c in consts],
        out_specs=[tok] * 7,
        scratch_shapes=[pltpu.VMEM((CONV_HALO, d), F32)],
        compiler_params=_params(("arbitrary", "arbitrary")),
        name="rwkv_proj",
    )(x, mod, *consts)
    return outs


PAIR = 2


def _rwkv_scan_kernel(r_ref, k_ref, v_ref, kk_ref, a_ref, ld_ref, rk_ref, y_ref, bonus_ref,
                      ht_s, lhs1_s, arb_s, ark_s, uv_s, v_s, bk_s, gl_s, *, nquad):
    step = pl.program_id(1)

    @pl.when(step == 0)
    def _():
        for ref in (ht_s, lhs1_s, arb_s, ark_s, uv_s, v_s, bk_s, gl_s):
            ref[...] = jnp.zeros(ref.shape, ref.dtype)

    row = lax.broadcasted_iota(jnp.int32, (QUAD, QUAD), 0) // HEAD
    col = lax.broadcasted_iota(jnp.int32, (QUAD, QUAD), 1) // HEAD
    blockmask = row == col
    t_i = lax.broadcasted_iota(jnp.int32, (CHUNK, QUAD), 0)
    s_i = lax.broadcasted_iota(jnp.int32, (CHUNK, QUAD), 1) % HEAD
    strict = s_i < t_i
    incl = s_i <= t_i
    blk16 = (s_i // 16) == (t_i // 16)
    off1 = ((s_i // 16) + 1 == (t_i // 16)) & ((t_i // 16) % 2 == 1)
    off2 = (s_i // 32 == 0) & (t_i // 32 == 1)
    eye_q = (s_i == t_i).astype(F32)
    ltri = (lax.broadcasted_iota(jnp.int32, (CHUNK, CHUNK), 1)
            <= lax.broadcasted_iota(jnp.int32, (CHUNK, CHUNK), 0)).astype(BF16)

    def bd(x):
        x4 = jnp.concatenate([x, x, x, x], axis=0)
        return jnp.where(blockmask, x4, 0.0).astype(BF16)

    def mm(a, b_bd):
        return _dot(a.astype(BF16), b_bd)

    def stack(a, b):
        return jnp.concatenate([a, b], axis=0).astype(BF16)

    probs = [(c, q) for c in range(PAIR) for q in range(nquad)]
    P = {}
    hts = [ht_s[q] for q in range(nquad)]
    S = {}

    def p1_prep():
        for c in range(PAIR):
            rows = slice(c * CHUNK, (c + 1) * CHUNK)
            ld_c = ld_ref[0, rows, :]
            hi = ld_c.astype(BF16)
            rem = ld_c - hi.astype(F32)
            mid = rem.astype(BF16)
            lo = (rem - mid.astype(F32)).astype(BF16)
            cum_c = _dot(ltri, hi) + _dot(ltri, mid) + _dot(ltri, lo)
            for q in range(nquad):
                sl = slice(q * QUAD, (q + 1) * QUAD)
                rc, kc, vc, kkc, ac = (ref[0, rows, sl] for ref in (r_ref, k_ref, v_ref, kk_ref, a_ref))
                cum, ldc = cum_c[:, sl], ld_c[:, sl]
                last = cum[CHUNK - 1:CHUNK, :]
                g_inv = jnp.exp(-cum)
                g_rem = jnp.exp(last - cum)
                bsrc = kkc * ac
                at = -kkc * jnp.exp(cum - ldc)
                rt = rc * jnp.exp(cum)
                P[c, q] = dict(at=at, rt=rt, v=vc, bt_bd=bd(bsrc * g_inv), kt_bd=bd(kc * g_inv),
                               rtk=(rt * rk_ref[0:1, sl]).astype(BF16),
                               bk=stack(bsrc * g_rem, kc * g_rem), gl=jnp.exp(last))

    def p1_amat():
        for p in probs:
            d = P[p]
            lhs = stack(d["at"], d["rt"])
            ab = _dot_nt(lhs, d.pop("bt_bd"))
            ak = _dot_nt(jnp.concatenate([lhs, d.pop("rtk")], axis=0), d.pop("kt_bd"))
            d["a_ab"] = jnp.where(strict, ab[0:CHUNK], 0.0)
            d["a_rb"] = jnp.where(incl, ab[CHUNK:], 0.0)
            d["a_ak"] = jnp.where(strict, ak[0:CHUNK], 0.0)
            d["a_rk"] = jnp.where(incl, ak[CHUNK:2 * CHUNK], 0.0)
            d["bonus"] = jnp.where(s_i == t_i, ak[2 * CHUNK:], 0.0)

    def p1_n1():
        for p in probs:
            d = P[p]
            ad = jnp.where(blk16, d["a_ab"], 0.0)
            d["p"] = eye_q + ad
            d["pw"] = mm(ad, bd(ad))

    def p1_n2():
        for p in probs:
            d = P[p]
            both = _dot(stack(d["p"], d["pw"]), bd(d["pw"]))
            d["p"] = d["p"] + both[0:CHUNK]
            d["pw"] = both[CHUNK:]

    def p1_n4():
        for p in probs:
            d = P[p]
            d["p"] = d["p"] + mm(d["p"], bd(d.pop("pw")))

    def p1_m1(mask):
        def run():
            for p in probs:
                d = P[p]
                d["p_bd"] = bd(d["p"])
                d["t"] = mm(d["p"], bd(jnp.where(mask, d["a_ab"], 0.0)))
        return run

    def p1_m2():
        for p in probs:
            d = P[p]
            d["p"] = d["p"] + mm(d.pop("t"), d.pop("p_bd"))

    def p1_akv():
        for (c, q) in probs:
            d = P[c, q]
            lhs = jnp.concatenate([d.pop("a_ak"), d.pop("bonus"), d.pop("a_rk")], axis=0).astype(BF16)
            prods = _dot(lhs, bd(d["v"]))
            d["akv"] = prods[0:CHUNK]
            bonus_ref[0, c * CHUNK:(c + 1) * CHUNK, q * QUAD:(q + 1) * QUAD] = prods[CHUNK:2 * CHUNK]
            d["arkv"] = prods[2 * CHUNK:]

    def p1_solve():
        for p in probs:
            d = P[p]
            w = d.pop("p").astype(BF16)
            d["wa"] = _dot(w, bd(d.pop("at")))
            d["uv"] = _dot(w, bd(d.pop("akv")))

    wr_base = (step % 2) * len(probs)
    rd_base = len(probs) - wr_base

    def p1_store():
        for n, p in enumerate(probs):
            d, i = P[p], wr_base + n
            lhs1_s[i] = stack(d["wa"], d["rt"])
            arb_s[i] = d["a_rb"].astype(BF16)
            ark_s[i] = d["arkv"]
            uv_s[i] = d["uv"]
            v_s[i] = d["v"]
            bk_s[i] = d["bk"]
            gl_s[i] = jnp.broadcast_to(d["gl"], (8, QUAD))

    def p2_a(c):
        def run():
            for q in range(nquad):
                i = rd_base + c * nquad + q
                uy = _dot_nt(lhs1_s[i], hts[q].astype(BF16))
                S[q] = dict(u=uy[0:CHUNK] + uv_s[i], y1=uy[CHUNK:])
        return run

    def p2_b(c):
        def run():
            rows = slice(c * CHUNK, (c + 1) * CHUNK)
            for q in range(nquad):
                i = rd_base + c * nquad + q
                d = S.pop(q)
                u, vc = d["u"], v_s[i]
                y_ref[0, rows, q * QUAD:(q + 1) * QUAD] = d["y1"] + _dot(arb_s[i], bd(u)) + ark_s[i]
                uv_t = jnp.concatenate([u, vc], axis=0).T.astype(BF16)
                upd = _dot(uv_t, bk_s[i])
                hts[q] = gl_s[i][0:1, :] * hts[q] + jnp.where(blockmask, upd, 0.0)
        return run

    def p2_store():
        for q in range(nquad):
            ht_s[q] = hts[q]

    for stage in (p1_prep, p2_a(0), p1_amat, p2_b(0), p1_n1, p2_a(1), p1_n2, p2_b(1), p2_store, p1_n2, p1_n4,
                  p1_m1(off1), p1_m2, p1_m1(off2), p1_m2, p1_akv, p1_solve, p1_store):
        stage()


def _rwkv_scan(r, k, v, kk, a, ld, r_k):
    b, s, d = r.shape
    nquad = d // QUAD
    span = PAIR * CHUNK
    npair = s // span
    nprob = 2 * PAIR * nquad
    blk_in = pl.BlockSpec((1, span, d), lambda i, c: (i, jnp.minimum(c, npair - 1), 0))
    blk_out = pl.BlockSpec((1, span, d), lambda i, c: (i, jnp.maximum(c - 1, 0), 0))
    return pl.pallas_call(
        functools.partial(_rwkv_scan_kernel, nquad=nquad),
        out_shape=[jax.ShapeDtypeStruct((b, s, d), F32)] * 2,
        grid=(b, npair + 1),
        in_specs=[blk_in] * 6 + [_const_spec((1, d))],
        out_specs=[blk_out, blk_in],
        scratch_shapes=[
            pltpu.VMEM((nquad, QUAD, QUAD), F32),
            pltpu.VMEM((nprob, 2 * CHUNK, QUAD), BF16),
            pltpu.VMEM((nprob, CHUNK, QUAD), BF16),
            pltpu.VMEM((nprob, CHUNK, QUAD), F32),
            pltpu.VMEM((nprob, CHUNK, QUAD), F32),
            pltpu.VMEM((nprob, CHUNK, QUAD), F32),
            pltpu.VMEM((nprob, 2 * CHUNK, QUAD), BF16),
            pltpu.VMEM((nprob, 8, QUAD), F32),
        ],
        compiler_params=_params(("arbitrary", "arbitrary")),
        name="rwkv_scan",
    )(r, k, v, kk, a, ld, r_k.reshape(1, d))


def _rwkv_out_kernel(x_ref, mod_ref, g_ref, y_ref, bonus_ref, gate_ref, vec_ref, ones_ref, wo_ref,
                     o_ref, *, ln_eps):
    x = x_ref[0]
    gt = mod_ref[0, 2:3, :]
    ln_g, ln_b = vec_ref[0:1, :], vec_ref[1:2, :]
    ones_bd = ones_ref[...]
    y = y_ref[0]
    mean = _group_sum(y, ones_bd) * (1.0 / HEAD)
    dlt = y - mean
    var = _group_sum(dlt * dlt, ones_bd) * (1.0 / HEAD)
    yn = dlt * lax.rsqrt(var + ln_eps) * ln_g + ln_b
    out = _dot(((yn + bonus_ref[0]) * gate_ref[0]).astype(BF16), wo_ref[...])
    o_ref[0] = x + gt * (_rms(out) * g_ref[1:2, :])


def _rwkv_out(x, mod, norm_g, y, bonus, gate, ln_g, ln_b, w_o, tm):
    b, s, d = x.shape
    vecs = jnp.stack([ln_g, ln_b], axis=0)
    tok = pl.BlockSpec((1, tm, d), lambda i, j: (i, j, 0))
    consts = [vecs, _head_ones(d), w_o.astype(BF16)]
    return pl.pallas_call(
        functools.partial(_rwkv_out_kernel, ln_eps=1e-5 * HEAD),
        out_shape=jax.ShapeDtypeStruct((b, s, d), F32),
        grid=(b, s // tm),
        in_specs=[tok, pl.BlockSpec((1,) + mod.shape[1:], lambda i, j: (i, 0, 0)), _const_spec(norm_g.shape)]
                 + [tok] * 3 + [_const_spec(c.shape) for c in consts],
        out_specs=tok,
        compiler_params=_params(("arbitrary", "arbitrary")),
        name="rwkv_out",
    )(x, mod, norm_g, y, bonus, gate, *consts)


def _pack_bf16_pairs(x):
    m = x.shape[1] // 2
    bits = lax.bitcast_convert_type(x.astype(BF16).astype(F32), jnp.uint32)
    return bits[:, :m] | (bits[:, m:] >> 16)


def _unpack_bf16_pairs(w):
    hi = lax.bitcast_convert_type(w & jnp.uint32(0xFFFF0000), F32)
    lo = lax.bitcast_convert_type(w << 16, F32)
    return jnp.concatenate([hi, lo], axis=1)


def _router_kernel(x_ref, mod_ref, g_ref, wr_ref, h_out, route_out, *, n_exp):
    x = x_ref[0]
    sh, sc = mod_ref[0, 3:4, :], mod_ref[0, 4:5, :]
    h = (_rms(x) * g_ref[2:3, :]) * (1.0 + sc) + sh
    h_out[0] = _pack_bf16_pairs(h)
    logits = _dot_f32(h, wr_ref[...])
    lane = lax.broadcasted_iota(jnp.int32, logits.shape, 1)
    neg = jnp.float32(-jnp.inf)
    logits = jnp.where(lane < n_exp, logits, neg)
    m1 = jnp.max(logits, axis=-1, keepdims=True)
    i1 = jnp.min(jnp.where(logits == m1, lane, LANE), axis=-1, keepdims=True)
    rest = jnp.where(lane == i1, neg, logits)
    m2 = jnp.max(rest, axis=-1, keepdims=True)
    i2 = jnp.min(jnp.where(rest == m2, lane, LANE), axis=-1, keepdims=True)
    w1 = 1.0 / (1.0 + jnp.exp(m2 - m1))
    w2 = 1.0 - w1
    route = jnp.where(lane == 0, i1.astype(F32), jnp.where(lane == 1, i2.astype(F32),
                      jnp.where(lane == 2, w1, jnp.where(lane == 3, w2, 0.0))))
    route_out[0] = route


def _router(x, mod, norm_g, w_router, tm):
    b, s, d = x.shape
    n_exp = w_router.shape[1]
    wr = _pad_cols(w_router, LANE)
    tok = pl.BlockSpec((1, tm, d), lambda i, j: (i, j, 0))
    return pl.pallas_call(
        functools.partial(_router_kernel, n_exp=n_exp),
        out_shape=[jax.ShapeDtypeStruct((b, s, d // 2), jnp.uint32), jax.ShapeDtypeStruct((b, s, LANE), F32)],
        grid=(b, s // tm),
        in_specs=[tok, pl.BlockSpec((1,) + mod.shape[1:], lambda i, j: (i, 0, 0)), _const_spec(norm_g.shape),
                  _const_spec(wr.shape)],
        out_specs=[pl.BlockSpec((1, tm, d // 2), lambda i, j: (i, j, 0)),
                   pl.BlockSpec((1, tm, LANE), lambda i, j: (i, j, 0))],
        compiler_params=_params(("arbitrary", "arbitrary")),
        name="moe_router",
    )(x, mod, norm_g, wr)


def _row_copy(src, src_row, dst, dst_row, sem):
    return pltpu.make_async_copy(src.at[pl.ds(src_row, 1)], dst.at[pl.ds(dst_row, 1)], sem)


def _expert_kernel(te_ref, src_cur, src_next, dst_prev, dst_cur, h_ref, wg_ref, wu_ref, wd_ref, out_ref,
                   xbuf, ybuf, gsem, ssem, *, tm):
    del te_ref
    i = pl.program_id(0)
    last = pl.num_programs(0) - 1
    cur, nxt = i % 2, (i + 1) % 2

    def gather(idx_ref, slot):
        return [_row_copy(h_ref, idx_ref[0, 0, r], xbuf.at[slot], r, gsem) for r in range(tm)]

    def scatter(idx_ref, slot):
        return [_row_copy(ybuf.at[slot], r, out_ref, idx_ref[0, 0, r], ssem) for r in range(tm)]

    def run(copies):
        for cp in copies:
            cp.start()
        for cp in copies:
            cp.wait()

    @pl.when(i == 0)
    def _():
        ybuf[...] = jnp.zeros(ybuf.shape, ybuf.dtype)
        run(gather(src_cur, 0))

    copies = gather(src_next, nxt) + scatter(dst_prev, nxt)
    for cp in copies:
        cp.start()
    h = _unpack_bf16_pairs(xbuf[cur]).astype(BF16)
    g = _dot(h, wg_ref[0])
    u = _dot(h, wu_ref[0])
    a = (g * jax.nn.sigmoid(g) * u).astype(BF16)
    ybuf[cur] = _pack_bf16_pairs(_dot(a, wd_ref[0]))
    for cp in copies:
        cp.wait()

    @pl.when(i == last)
    def _():
        run(scatter(dst_cur, cur))


def _experts(h2, tile_expert, src, dst, n_out, wg, wu, wd, tm):
    d, f = wg.shape[1], wg.shape[2]
    dp = h2.shape[1]
    n_tiles = src.shape[0] // tm
    src3, dst3 = src.reshape(n_tiles, 1, tm), dst.reshape(n_tiles, 1, tm)

    def idx_spec(fn):
        return pl.BlockSpec((1, 1, tm), lambda i, te: (fn(i), 0, 0), memory_space=pltpu.SMEM)

    def w_spec(shape):
        return pl.BlockSpec((1,) + shape, lambda i, te: (te[i], 0, 0), pipeline_mode=pl.Buffered(1))

    grid_spec = pltpu.PrefetchScalarGridSpec(
        num_scalar_prefetch=1,
        grid=(n_tiles,),
        in_specs=[
            idx_spec(lambda i: i),
            idx_spec(lambda i: jnp.minimum(i + 1, n_tiles - 1)),
            idx_spec(lambda i: jnp.maximum(i - 1, 0)),
            idx_spec(lambda i: i),
            pl.BlockSpec(memory_space=pl.ANY),
            w_spec((d, f)), w_spec((d, f)), w_spec((f, d)),
        ],
        out_specs=pl.BlockSpec(memory_space=pl.ANY),
        scratch_shapes=[pltpu.VMEM((2, tm, dp), jnp.uint32), pltpu.VMEM((2, tm, dp), jnp.uint32),
                        pltpu.SemaphoreType.DMA(()), pltpu.SemaphoreType.DMA(())],
    )
    return pl.pallas_call(
        functools.partial(_expert_kernel, tm=tm),
        out_shape=jax.ShapeDtypeStruct((n_out, dp), jnp.uint32),
        grid_spec=grid_spec,
        compiler_params=_params(("arbitrary",)),
        name="moe_experts",
    )(tile_expert, src3, src3, dst3, dst3, h2, wg.astype(BF16), wu.astype(BF16), wd.astype(BF16))


def _combine_kernel(x_ref, mod_ref, g_ref, route_ref, y0_ref, y1_ref, o_ref):
    x = x_ref[0]
    gt = mod_ref[0, 5:6, :]
    route = route_ref[0]
    y = route[:, 2:3] * _unpack_bf16_pairs(y0_ref[...]) + route[:, 3:4] * _unpack_bf16_pairs(y1_ref[...])
    o_ref[0] = x + gt * (_rms(y) * g_ref[3:4, :])


def _combine(x, mod, norm_g, route, ys, tm):
    b, s, d = x.shape
    nj = s // tm
    tok = pl.BlockSpec((1, tm, d), lambda i, j: (i, j, 0))
    return pl.pallas_call(
        _combine_kernel,
        out_shape=jax.ShapeDtypeStruct((b, s, d), F32),
        grid=(b, nj),
        in_specs=[
            tok,
            pl.BlockSpec((1,) + mod.shape[1:], lambda i, j: (i, 0, 0)),
            _const_spec(norm_g.shape),
            pl.BlockSpec((1, tm, LANE), lambda i, j: (i, j, 0)),
            pl.BlockSpec((tm, d // 2), lambda i, j: (i * nj + j, 0)),
            pl.BlockSpec((tm, d // 2), lambda i, j: (b * nj + i * nj + j, 0)),
        ],
        out_specs=tok,
        compiler_params=_params(("arbitrary", "arbitrary")),
        name="moe_combine",
    )(x, mod, norm_g, route, ys, ys)


def _route_maps(idx, n_exp, tm):
    flat = idx.reshape(-1)
    n_pairs = flat.shape[0]
    n_tok = n_pairs // TOP_K
    onehot = (flat[:, None] == jnp.arange(n_exp, dtype=jnp.int32)[None, :]).astype(jnp.int32)
    csum = jnp.cumsum(onehot, axis=0)
    rank = jnp.sum(onehot * csum, axis=1) - 1
    counts = csum[-1]
    tiles = (counts + tm - 1) // tm
    tile_end = jnp.cumsum(tiles)
    tile_start = tile_end - tiles
    pos = tile_start[flat] * tm + rank
    n_tiles = n_pairs // tm + n_exp
    tile_ids = jnp.arange(n_tiles, dtype=jnp.int32)
    tile_expert = jnp.minimum(jnp.sum((tile_end[None, :] <= tile_ids[:, None]).astype(jnp.int32), axis=1),
                              n_exp - 1).astype(jnp.int32)
    n_rows = n_tiles * tm
    pair = jnp.full((n_rows,), -1, jnp.int32).at[pos].set(jnp.arange(n_pairs, dtype=jnp.int32),
                                                         unique_indices=True)
    valid = pair >= 0
    tok, slot = pair // TOP_K, pair % TOP_K
    src = jnp.where(valid, tok, 0).astype(jnp.int32)
    spare = n_pairs + jnp.arange(n_rows, dtype=jnp.int32) % tm
    dst = jnp.where(valid, slot * n_tok + tok, spare).astype(jnp.int32)
    return tile_expert, src, dst, n_pairs + tm


def _moe(x, mod, norm_g, w_router, wg, wu, wd, tm, tm_e):
    b, s, d = x.shape
    n_exp = w_router.shape[1]
    h2, route = _router(x, mod, norm_g, w_router, tm)
    idx = route[..., 0:TOP_K].astype(jnp.int32).reshape(b * s, TOP_K)
    tile_expert, src, dst, n_out = _route_maps(idx, n_exp, tm_e)
    ys = _experts(h2.reshape(b * s, d // 2), tile_expert, src, dst, n_out, wg, wu, wd, tm_e)
    return _combine(x, mod, norm_g, route, ys, tm)


def _tile(s, want):
    t = min(want, s)
    assert s % t == 0 and t % CHUNK == 0
    return t


def kernel(x, c, ada_w, ada_b, norm_g, mix_w_in, conv_w, pool_w, pool_scale, mix_w_out, ffn_w_gate, ffn_w_up, ffn_w_down, rwkv_mu, rwkv_w_r, rwkv_w_k, rwkv_w_v, rwkv_w_o, rwkv_w0, rwkv_w1, rwkv_w2, rwkv_a0, rwkv_a1, rwkv_a2, rwkv_g1, rwkv_g2, rwkv_k_k, rwkv_k_a, rwkv_r_k, rwkv_ln_g, rwkv_ln_b, moe_router, moe_w_gate, moe_w_up, moe_w_down):
    depth = ada_w.shape[0]
    s = x.shape[1]
    tm = _tile(s, 512)
    mods = _ada(c, ada_w, ada_b)
    for layer in range(depth):
        mod, ng, i = mods[layer], norm_g[layer], layer // 2
        if layer % 2 == 0:
            x = _mixer(x, mod, ng, mix_w_in[i], conv_w[i], pool_w[i], pool_scale[i], mix_w_out[i], tm)
            x = _ffn(x, mod, ng, ffn_w_gate[i], ffn_w_up[i], ffn_w_down[i], tm)
        else:
            r, k, v, kk, a, ld, gate = _rwkv_proj(
                x, mod, ng, rwkv_mu[i], rwkv_w_r[i], rwkv_w_k[i], rwkv_w_v[i], rwkv_w0[i], rwkv_w1[i], rwkv_w2[i],
                rwkv_a0[i], rwkv_a1[i], rwkv_a2[i], rwkv_g1[i], rwkv_g2[i], rwkv_k_k[i], rwkv_k_a[i], tm)
            y, bonus = _rwkv_scan(r, k, v, kk, a, ld, rwkv_r_k[i])
            x = _rwkv_out(x, mod, ng, y, bonus, gate, rwkv_ln_g[i], rwkv_ln_b[i], rwkv_w_o[i], tm)
            x = _moe(x, mod, ng, moe_router[i], moe_w_gate[i], moe_w_up[i], moe_w_down[i], tm, tm)
    return x
```

```python
import functools

import jax
import jax.numpy as jnp
from jax import lax
from jax.experimental import pallas as pl
from jax.experimental.pallas import tpu as pltpu

F32 = jnp.float32
BF16 = jnp.bfloat16

RMS_EPS = 1e-6
DECAY_SCALE = 0.6065306597126334
HEAD = 64
QUAD = 4 * HEAD
CHUNK = 64
POOL_WINDOWS = (2, 4, 8, 16)
MAX_WIN = 16
CONV_HALO = 8
LANE = 128
TOP_K = 2
PAIR_CHUNK = 4096
PAIR_UNROLL = 16
VMEM_LIMIT = 56 * 1024 * 1024


def _dot(a, b):
    return jnp.dot(a, b, preferred_element_type=F32)


def _dot_nt(a, b):
    return lax.dot_general(a, b, (((1,), (1,)), ((), ())), preferred_element_type=F32)


def _dot_f32(a, b):
    return jnp.dot(a, b, preferred_element_type=F32, precision=lax.Precision.HIGHEST)


def _rms(x):
    return x * lax.rsqrt(jnp.mean(x * x, axis=-1, keepdims=True) + RMS_EPS)


def _params(sem):
    return pltpu.CompilerParams(dimension_semantics=sem, vmem_limit_bytes=VMEM_LIMIT)


def _const_spec(shape):
    nd = len(shape)
    return pl.BlockSpec(shape, lambda *_: (0,) * nd, pipeline_mode=pl.Buffered(1))


def _ada_kernel(c_ref, w_ref, b_ref, o_ref):
    c = c_ref[...]
    cond = c * jax.nn.sigmoid(c)
    o_ref[0] = _dot_f32(cond, w_ref[0]) + b_ref[0]


def _ada(c, ada_w, ada_b):
    depth, d, d6 = ada_w.shape
    b = c.shape[0]
    nblk = d6 // d
    out = pl.pallas_call(
        _ada_kernel,
        out_shape=jax.ShapeDtypeStruct((depth, b, d6), F32),
        grid=(depth, nblk),
        in_specs=[
            pl.BlockSpec((b, d), lambda l, n: (0, 0)),
            pl.BlockSpec((1, d, d), lambda l, n: (l, 0, n)),
            pl.BlockSpec((1, 1, d), lambda l, n: (l, 0, n)),
        ],
        out_specs=pl.BlockSpec((1, b, d), lambda l, n: (l, 0, n)),
        compiler_params=_params(("arbitrary", "arbitrary")),
        name="ada_mod",
    )(c, ada_w, ada_b.reshape(depth, 1, d6))
    return out.reshape(depth, b, nblk, d)


def _mix_kernel(x_ref, mod_ref, g_ref, w_in_ref, conv_w_ref, pool_w_ref, pool_s_ref, w_out_ref,
                o_ref, u_halo, v_halo, *, tm, dc, gc):
    j = pl.program_id(1)

    @pl.when(j == 0)
    def _():
        u_halo[...] = jnp.zeros(u_halo.shape, F32)
        v_halo[...] = jnp.zeros(v_halo.shape, F32)

    x = x_ref[0]
    sh, sc, gt = mod_ref[0, 0:1, :], mod_ref[0, 1:2, :], mod_ref[0, 2:3, :]
    h = (_rms(x) * g_ref[0:1, :]) * (1.0 + sc) + sh
    z = _dot(h.astype(BF16), w_in_ref[...])
    b_gate, c_gate, v_conv, v_pool = z[:, 0:dc], z[:, dc:2 * dc], z[:, 2 * dc:3 * dc], z[:, 3 * dc:]

    u = c_gate * v_conv
    u_ext = jnp.concatenate([u_halo[...], u], axis=0)
    um1 = pltpu.roll(u_ext, 1, 0)[CONV_HALO:]
    um2 = pltpu.roll(u_ext, 2, 0)[CONV_HALO:]
    y_conv = b_gate * (um2 * conv_w_ref[0:1, :] + um1 * conv_w_ref[1:2, :] + u * conv_w_ref[2:3, :])
    u_halo[...] = u[tm - CONV_HALO:tm, :]

    sums = jnp.concatenate([v_halo[...], v_pool], axis=0)
    pos = j * tm + lax.broadcasted_iota(jnp.int32, (tm, 1), 0)
    pieces = [y_conv.astype(BF16)]
    width = 1
    for gi, win in enumerate(POOL_WINDOWS):
        assert win == 2 * width
        lo, hi = gi * gc, (gi + 1) * gc
        sums = sums + pltpu.roll(sums, width, 0)
        width = win
        cnt = jnp.minimum(pos + 1, win).astype(F32)
        p = sums[MAX_WIN:, 0:gc] / cnt - v_pool[:, lo:hi]
        sums = sums[:, gc:]
        yp = _dot(p.astype(BF16), pool_w_ref[gi]) * pool_s_ref[0:1, lo:hi]
        pieces.append(yp.astype(BF16))
    v_halo[...] = v_pool[tm - MAX_WIN:tm, :]

    y = _dot(jnp.concatenate(pieces, axis=-1), w_out_ref[...])
    o_ref[0] = x + gt * (_rms(y) * g_ref[1:2, :])


def _mixer(x, mod, norm_g, w_in, conv_w, pool_w, pool_scale, w_out, tm):
    b, s, d = x.shape
    dc = conv_w.shape[1]
    ng, gc = pool_w.shape[0], pool_w.shape[1]
    kern = functools.partial(_mix_kernel, tm=tm, dc=dc, gc=gc)
    return pl.pallas_call(
        kern,
        out_shape=jax.ShapeDtypeStruct((b, s, d), F32),
        grid=(b, s // tm),
        in_specs=[
            pl.BlockSpec((1, tm, d), lambda i, j: (i, j, 0)),
            pl.BlockSpec((1,) + mod.shape[1:], lambda i, j: (i, 0, 0)),
            _const_spec(norm_g.shape),
            _const_spec(w_in.shape),
            _const_spec(conv_w.shape),
            _const_spec(pool_w.shape),
            _const_spec((1, ng * gc)),
            _const_spec(w_out.shape),
        ],
        out_specs=pl.BlockSpec((1, tm, d), lambda i, j: (i, j, 0)),
        scratch_shapes=[pltpu.VMEM((CONV_HALO, dc), F32), pltpu.VMEM((MAX_WIN, ng * gc), F32)],
        compiler_params=_params(("arbitrary", "arbitrary")),
        name="conv_pool_mixer",
    )(x, mod, norm_g, w_in.astype(BF16), conv_w, pool_w.astype(BF16), pool_scale.reshape(1, -1),
      w_out.astype(BF16))


def _ffn_kernel(x_ref, mod_ref, g_ref, wg_ref, wu_ref, wd_ref, o_ref):
    x = x_ref[0]
    sh, sc, gt = mod_ref[0, 3:4, :], mod_ref[0, 4:5, :], mod_ref[0, 5:6, :]
    h = ((_rms(x) * g_ref[2:3, :]) * (1.0 + sc) + sh).astype(BF16)
    g = _dot(h, wg_ref[...])
    u = _dot(h, wu_ref[...])
    a = (g * jax.nn.sigmoid(g) * u).astype(BF16)
    y = _dot(a, wd_ref[...])
    o_ref[0] = x + gt * (_rms(y) * g_ref[3:4, :])


def _ffn(x, mod, norm_g, wg, wu, wd, tm):
    b, s, d = x.shape
    return pl.pallas_call(
        _ffn_kernel,
        out_shape=jax.ShapeDtypeStruct((b, s, d), F32),
        grid=(b, s // tm),
        in_specs=[
            pl.BlockSpec((1, tm, d), lambda i, j: (i, j, 0)),
            pl.BlockSpec((1,) + mod.shape[1:], lambda i, j: (i, 0, 0)),
            _const_spec(norm_g.shape),
            _const_spec(wg.shape),
            _const_spec(wu.shape),
            _const_spec(wd.shape),
        ],
        out_specs=pl.BlockSpec((1, tm, d), lambda i, j: (i, j, 0)),
        compiler_params=_params(("arbitrary", "arbitrary")),
        name="dense_swiglu",
    )(x, mod, norm_g, wg.astype(BF16), wu.astype(BF16), wd.astype(BF16))


def _group_sum(x, ones_bd):
    return _dot(x.astype(BF16), ones_bd)


def _rwkv_proj_kernel(x_ref, mod_ref, g_ref, mu_ref, wr_ref, wk_ref, wv_ref, w1_ref, w2_ref, a1_ref, a2_ref,
                      g1_ref, g2_ref, vec_ref, ones_ref,
                      r_out, k_out, v_out, kk_out, a_out, ld_out, g_out, h_halo, *, tm):
    j = pl.program_id(1)

    @pl.when(j == 0)
    def _():
        h_halo[...] = jnp.zeros(h_halo.shape, F32)

    x = x_ref[0]
    sh, sc = mod_ref[0, 0:1, :], mod_ref[0, 1:2, :]
    h = (_rms(x) * g_ref[0:1, :]) * (1.0 + sc) + sh
    h_prev = pltpu.roll(jnp.concatenate([h_halo[...], h], axis=0), 1, 0)[CONV_HALO:]
    h_halo[...] = h[tm - CONV_HALO:tm, :]
    h_bf = h.astype(BF16)
    xx_bf = (h_prev - h).astype(BF16)

    def mixed(i):
        return h_bf + xx_bf * mu_ref[i:i + 1, :].astype(BF16)

    w0, a0, k_k, k_a = vec_ref[0:1, :], vec_ref[1:2, :], vec_ref[2:3, :], vec_ref[3:4, :]
    r = _dot(mixed(0), wr_ref[...])
    w_lora = _dot(jnp.tanh(_dot(mixed(1), w1_ref[...])).astype(BF16), w2_ref[...])
    k = _dot(mixed(2), wk_ref[...])
    v = _dot(mixed(3), wv_ref[...])
    a_lora = _dot(_dot(mixed(4), a1_ref[...]).astype(BF16), a2_ref[...])
    gate = _dot(jax.nn.sigmoid(_dot(mixed(5), g1_ref[...])).astype(BF16), g2_ref[...])

    log_decay = -DECAY_SCALE * jax.nn.sigmoid(w0 + w_lora)
    a = jax.nn.sigmoid(a0 + a_lora)
    kk = k * k_k
    nrm = jnp.sqrt(_group_sum(kk * kk, ones_ref[...]))
    kk = kk / jnp.maximum(nrm, 1e-12)

    r_out[0] = r
    k_out[0] = k * (1.0 + (a - 1.0) * k_a)
    v_out[0] = v
    kk_out[0] = kk
    a_out[0] = a
    ld_out[0] = log_decay
    g_out[0] = gate


def _pad_cols(w, n):
    return jnp.pad(w, ((0, 0), (0, n - w.shape[1])))


def _pad_rows(w, n):
    return jnp.pad(w, ((0, n - w.shape[0]), (0, 0)))


def _ceil_to(n, m):
    return -(-n // m) * m


def _head_ones(d):
    idx = jnp.arange(d) // HEAD
    return (idx[:, None] == idx[None, :]).astype(BF16)


def _rwkv_proj(x, mod, norm_g, mu, w_r, w_k, w_v, w0, w1, w2, a0, a1, a2, g1, g2, k_k, k_a, tm):
    b, s, d = x.shape
    lw, la, lg = (_ceil_to(w.shape[1], LANE) for w in (w1, a1, g1))
    w1p, w2p = _pad_cols(w1, lw).astype(BF16), _pad_rows(w2, lw).astype(BF16)
    a1p, a2p = _pad_cols(a1, la).astype(BF16), _pad_rows(a2, la).astype(BF16)
    g1p, g2p = _pad_cols(g1, lg).astype(BF16), _pad_rows(g2, lg).astype(BF16)
    vecs = jnp.stack([w0, a0, k_k, k_a], axis=0)
    consts = [norm_g, mu, w_r.astype(BF16), w_k.astype(BF16), w_v.astype(BF16), w1p, w2p, a1p, a2p, g1p, g2p,
              vecs, _head_ones(d)]
    tok = pl.BlockSpec((1, tm, d), lambda i, j: (i, j, 0))
    outs = pl.pallas_call(
        functools.partial(_rwkv_proj_kernel, tm=tm),
        out_shape=[jax.ShapeDtypeStruct((b, s, d), F32)] * 7,
        grid=(b, s // tm),
        in_specs=[tok, pl.BlockSpec((1,) + mod.shape[1:], lambda i, j: (i, 0, 0))]
                 + [_const_spec(c.shape) for c in consts],
        out_specs=[tok] * 7,
        scratch_shapes=[pltpu.VMEM((CONV_HALO, d), F32)],
        compiler_params=_params(("arbitrary", "arbitrary")),
        name="rwkv_proj",
    )(x, mod, *consts)
    return outs


PAIR = 2


def _rwkv_scan_kernel(r_ref, k_ref, v_ref, kk_ref, a_ref, ld_ref, rk_ref, y_ref, bonus_ref,
                      ht_s, lhs1_s, arb_s, ark_s, uv_s, v_s, bk_s, gl_s, *, nquad):
    step = pl.program_id(1)

    @pl.when(step == 0)
    def _():
        for ref in (ht_s, lhs1_s, arb_s, ark_s, uv_s, v_s, bk_s, gl_s):
            ref[...] = jnp.zeros(ref.shape, ref.dtype)

    row = lax.broadcasted_iota(jnp.int32, (QUAD, QUAD), 0) // HEAD
    col = lax.broadcasted_iota(jnp.int32, (QUAD, QUAD), 1) // HEAD
    blockmask = row == col
    t_i = lax.broadcasted_iota(jnp.int32, (CHUNK, QUAD), 0)
    s_i = lax.broadcasted_iota(jnp.int32, (CHUNK, QUAD), 1) % HEAD
    strict = s_i < t_i
    incl = s_i <= t_i
    blk16 = (s_i // 16) == (t_i // 16)
    off1 = ((s_i // 16) + 1 == (t_i // 16)) & ((t_i // 16) % 2 == 1)
    off2 = (s_i // 32 == 0) & (t_i // 32 == 1)
    eye_q = (s_i == t_i).astype(F32)
    ltri = (lax.broadcasted_iota(jnp.int32, (CHUNK, CHUNK), 1)
            <= lax.broadcasted_iota(jnp.int32, (CHUNK, CHUNK), 0)).astype(BF16)

    def bd(x):
        x4 = jnp.concatenate([x, x, x, x], axis=0)
        return jnp.where(blockmask, x4, 0.0).astype(BF16)

    def mm(a, b_bd):
        return _dot(a.astype(BF16), b_bd)

    def stack(a, b):
        return jnp.concatenate([a, b], axis=0).astype(BF16)

    probs = [(c, q) for c in range(PAIR) for q in range(nquad)]
    P = {}
    hts = [ht_s[q] for q in range(nquad)]
    S = {}

    def p1_prep():
        for c in range(PAIR):
            rows = slice(c * CHUNK, (c + 1) * CHUNK)
            ld_c = ld_ref[0, rows, :]
            hi = ld_c.astype(BF16)
            rem = ld_c - hi.astype(F32)
            mid = rem.astype(BF16)
            lo = (rem - mid.astype(F32)).astype(BF16)
            cum_c = _dot(ltri, hi) + _dot(ltri, mid) + _dot(ltri, lo)
            for q in range(nquad):
                sl = slice(q * QUAD, (q + 1) * QUAD)
                rc, kc, vc, kkc, ac = (ref[0, rows, sl] for ref in (r_ref, k_ref, v_ref, kk_ref, a_ref))
                cum, ldc = cum_c[:, sl], ld_c[:, sl]
                last = cum[CHUNK - 1:CHUNK, :]
                g_inv = jnp.exp(-cum)
                g_rem = jnp.exp(last - cum)
                bsrc = kkc * ac
                at = -kkc * jnp.exp(cum - ldc)
                rt = rc * jnp.exp(cum)
                P[c, q] = dict(at=at, rt=rt, v=vc, bt_bd=bd(bsrc * g_inv), kt_bd=bd(kc * g_inv),
                               rtk=(rt * rk_ref[0:1, sl]).astype(BF16),
                               bk=stack(bsrc * g_rem, kc * g_rem), gl=jnp.exp(last))

    def p1_amat():
        for p in probs:
            d = P[p]
            lhs = stack(d["at"], d["rt"])
            ab = _dot_nt(lhs, d.pop("bt_bd"))
            ak = _dot_nt(jnp.concatenate([lhs, d.pop("rtk")], axis=0), d.pop("kt_bd"))
            d["a_ab"] = jnp.where(strict, ab[0:CHUNK], 0.0)
            d["a_rb"] = jnp.where(incl, ab[CHUNK:], 0.0)
            d["a_ak"] = jnp.where(strict, ak[0:CHUNK], 0.0)
            d["a_rk"] = jnp.where(incl, ak[CHUNK:2 * CHUNK], 0.0)
            d["bonus"] = jnp.where(s_i == t_i, ak[2 * CHUNK:], 0.0)

    def p1_n1():
        for p in probs:
            d = P[p]
            ad = jnp.where(blk16, d["a_ab"], 0.0)
            d["p"] = eye_q + ad
            d["pw"] = mm(ad, bd(ad))

    def p1_n2():
        for p in probs:
            d = P[p]
            both = _dot(stack(d["p"], d["pw"]), bd(d["pw"]))
            d["p"] = d["p"] + both[0:CHUNK]
            d["pw"] = both[CHUNK:]

    def p1_n4():
        for p in probs:
            d = P[p]
            d["p"] = d["p"] + mm(d["p"], bd(d.pop("pw")))

    def p1_m1(mask):
        def run():
            for p in probs:
                d = P[p]
                d["p_bd"] = bd(d["p"])
                d["t"] = mm(d["p"], bd(jnp.where(mask, d["a_ab"], 0.0)))
        return run

    def p1_m2():
        for p in probs:
            d = P[p]
            d["p"] = d["p"] + mm(d.pop("t"), d.pop("p_bd"))

    def p1_akv():
        for (c, q) in probs:
            d = P[c, q]
            lhs = jnp.concatenate([d.pop("a_ak"), d.pop("bonus"), d.pop("a_rk")], axis=0).astype(BF16)
            prods = _dot(lhs, bd(d["v"]))
            d["akv"] = prods[0:CHUNK]
            bonus_ref[0, c * CHUNK:(c + 1) * CHUNK, q * QUAD:(q + 1) * QUAD] = prods[CHUNK:2 * CHUNK]
            d["arkv"] = prods[2 * CHUNK:]

    def p1_solve():
        for p in probs:
            d = P[p]
            w = d.pop("p").astype(BF16)
            d["wa"] = _dot(w, bd(d.pop("at")))
            d["uv"] = _dot(w, bd(d.pop("akv")))

    wr_base = (step % 2) * len(probs)
    rd_base = len(probs) - wr_base

    def p1_store():
        for n, p in enumerate(probs):
            d, i = P[p], wr_base + n
            lhs1_s[i] = stack(d["wa"], d["rt"])
            arb_s[i] = d["a_rb"].astype(BF16)
            ark_s[i] = d["arkv"]
            uv_s[i] = d["uv"]
            v_s[i] = d["v"]
            bk_s[i] = d["bk"]
            gl_s[i] = jnp.broadcast_to(d["gl"], (8, QUAD))

    def p2_a(c):
        def run():
            for q in range(nquad):
                i = rd_base + c * nquad + q
                uy = _dot_nt(lhs1_s[i], hts[q].astype(BF16))
                S[q] = dict(u=uy[0:CHUNK] + uv_s[i], y1=uy[CHUNK:])
        return run

    def p2_b(c):
        def run():
            rows = slice(c * CHUNK, (c + 1) * CHUNK)
            for q in range(nquad):
                i = rd_base + c * nquad + q
                d = S.pop(q)
                u, vc = d["u"], v_s[i]
                y_ref[0, rows, q * QUAD:(q + 1) * QUAD] = d["y1"] + _dot(arb_s[i], bd(u)) + ark_s[i]
                uv_t = jnp.concatenate([u, vc], axis=0).T.astype(BF16)
                upd = _dot(uv_t, bk_s[i])
                hts[q] = gl_s[i][0:1, :] * hts[q] + jnp.where(blockmask, upd, 0.0)
        return run

    def p2_store():
        for q in range(nquad):
            ht_s[q] = hts[q]

    for stage in (p1_prep, p2_a(0), p1_amat, p2_b(0), p1_n1, p2_a(1), p1_n2, p2_b(1), p2_store, p1_n2, p1_n4,
                  p1_m1(off1), p1_m2, p1_m1(off2), p1_m2, p1_akv, p1_solve, p1_store):
        stage()


def _rwkv_scan(r, k, v, kk, a, ld, r_k):
    b, s, d = r.shape
    nquad = d // QUAD
    span = PAIR * CHUNK
    npair = s // span
    nprob = 2 * PAIR * nquad
    blk_in = pl.BlockSpec((1, span, d), lambda i, c: (i, jnp.minimum(c, npair - 1), 0))
    blk_out = pl.BlockSpec((1, span, d), lambda i, c: (i, jnp.maximum(c - 1, 0), 0))
    return pl.pallas_call(
        functools.partial(_rwkv_scan_kernel, nquad=nquad),
        out_shape=[jax.ShapeDtypeStruct((b, s, d), F32)] * 2,
        grid=(b, npair + 1),
        in_specs=[blk_in] * 6 + [_const_spec((1, d))],
        out_specs=[blk_out, blk_in],
        scratch_shapes=[
            pltpu.VMEM((nquad, QUAD, QUAD), F32),
            pltpu.VMEM((nprob, 2 * CHUNK, QUAD), BF16),
            pltpu.VMEM((nprob, CHUNK, QUAD), BF16),
            pltpu.VMEM((nprob, CHUNK, QUAD), F32),
            pltpu.VMEM((nprob, CHUNK, QUAD), F32),
            pltpu.VMEM((nprob, CHUNK, QUAD), F32),
            pltpu.VMEM((nprob, 2 * CHUNK, QUAD), BF16),
            pltpu.VMEM((nprob, 8, QUAD), F32),
        ],
        compiler_params=_params(("arbitrary", "arbitrary")),
        name="rwkv_scan",
    )(r, k, v, kk, a, ld, r_k.reshape(1, d))


def _route_top2(h, w_router, n_exp):
    logits = _dot_f32(h, w_router)
    lane = lax.broadcasted_iota(jnp.int32, logits.shape, 1)
    neg = jnp.float32(-jnp.inf)
    logits = jnp.where(lane < n_exp, logits, neg)
    m1 = jnp.max(logits, axis=-1, keepdims=True)
    i1 = jnp.min(jnp.where(logits == m1, lane, LANE), axis=-1, keepdims=True)
    rest = jnp.where(lane == i1, neg, logits)
    m2 = jnp.max(rest, axis=-1, keepdims=True)
    i2 = jnp.min(jnp.where(rest == m2, lane, LANE), axis=-1, keepdims=True)
    w1 = 1.0 / (1.0 + jnp.exp(m2 - m1))
    w2 = 1.0 - w1
    return jnp.where(lane == 0, i1.astype(F32), jnp.where(lane == 1, i2.astype(F32),
                     jnp.where(lane == 2, w1, jnp.where(lane == 3, w2, 0.0))))


def _rwkv_out_kernel(x_ref, mod_ref, g_ref, y_ref, bonus_ref, gate_ref, vec_ref, ones_ref, wo_ref,
                     o_ref, *, ln_eps):
    x = x_ref[0]
    gt = mod_ref[0, 2:3, :]
    ln_g, ln_b = vec_ref[0:1, :], vec_ref[1:2, :]
    ones_bd = ones_ref[...]
    y = y_ref[0]
    mean = _group_sum(y, ones_bd) * (1.0 / HEAD)
    dlt = y - mean
    var = _group_sum(dlt * dlt, ones_bd) * (1.0 / HEAD)
    yn = dlt * lax.rsqrt(var + ln_eps) * ln_g + ln_b
    out = _dot(((yn + bonus_ref[0]) * gate_ref[0]).astype(BF16), wo_ref[...])
    o_ref[0] = x + gt * (_rms(out) * g_ref[1:2, :])


def _rwkv_out(x, mod, norm_g, y, bonus, gate, ln_g, ln_b, w_o, tm):
    b, s, d = x.shape
    vecs = jnp.stack([ln_g, ln_b], axis=0)
    tok = pl.BlockSpec((1, tm, d), lambda i, j: (i, j, 0))
    consts = [vecs, _head_ones(d), w_o.astype(BF16)]
    return pl.pallas_call(
        functools.partial(_rwkv_out_kernel, ln_eps=1e-5 * HEAD),
        out_shape=jax.ShapeDtypeStruct((b, s, d), F32),
        grid=(b, s // tm),
        in_specs=[tok, pl.BlockSpec((1,) + mod.shape[1:], lambda i, j: (i, 0, 0)), _const_spec(norm_g.shape)]
                 + [tok] * 3 + [_const_spec(c.shape) for c in consts],
        out_specs=tok,
        compiler_params=_params(("arbitrary", "arbitrary")),
        name="rwkv_out",
    )(x, mod, norm_g, y, bonus, gate, *consts)


def _router_kernel(x_ref, mod_ref, g_ref, wr_ref, h_out, route_out, *, n_exp):
    x = x_ref[0]
    sh, sc = mod_ref[0, 3:4, :], mod_ref[0, 4:5, :]
    h = (_rms(x) * g_ref[2:3, :]) * (1.0 + sc) + sh
    h_out[0] = h
    route_out[0] = _route_top2(h, wr_ref[...], n_exp)


def _router(x, mod, norm_g, w_router, tm):
    b, s, d = x.shape
    wr = _pad_cols(w_router, LANE)
    tok = pl.BlockSpec((1, tm, d), lambda i, j: (i, j, 0))
    return pl.pallas_call(
        functools.partial(_router_kernel, n_exp=w_router.shape[1]),
        out_shape=[jax.ShapeDtypeStruct((b, s, d), F32), jax.ShapeDtypeStruct((b, s, LANE), F32)],
        grid=(b, s // tm),
        in_specs=[tok, pl.BlockSpec((1,) + mod.shape[1:], lambda i, j: (i, 0, 0)), _const_spec(norm_g.shape),
                  _const_spec(wr.shape)],
        out_specs=[tok, pl.BlockSpec((1, tm, LANE), lambda i, j: (i, j, 0))],
        compiler_params=_params(("arbitrary", "arbitrary")),
        name="moe_router",
    )(x, mod, norm_g, wr)


def _row_copy(src, src_row, dst, dst_row, sem):
    return pltpu.make_async_copy(src.at[pl.ds(src_row, 1)], dst.at[pl.ds(dst_row, 1)], sem)


def _expert_kernel(te_ref, src_cur, src_next, dst_prev, dst_cur, h_ref, wg_ref, wu_ref, wd_ref, out_ref,
                   xbuf, ybuf, gsem, ssem, *, tm):
    del te_ref
    i = pl.program_id(0)
    last = pl.num_programs(0) - 1
    cur, nxt = i % 2, (i + 1) % 2

    def gather(idx_ref, slot):
        return [_row_copy(h_ref, idx_ref[0, 0, r], xbuf.at[slot], r, gsem) for r in range(tm)]

    def scatter(idx_ref, slot):
        return [_row_copy(ybuf.at[slot], r, out_ref, idx_ref[0, 0, r], ssem) for r in range(tm)]

    def run(copies):
        for cp in copies:
            cp.start()
        for cp in copies:
            cp.wait()

    @pl.when(i == 0)
    def _():
        ybuf[...] = jnp.zeros(ybuf.shape, ybuf.dtype)
        run(gather(src_cur, 0))

    copies = gather(src_next, nxt) + scatter(dst_prev, nxt)
    for cp in copies:
        cp.start()
    h = xbuf[cur].astype(BF16)
    g = _dot(h, wg_ref[0])
    u = _dot(h, wu_ref[0])
    a = (g * jax.nn.sigmoid(g) * u).astype(BF16)
    ybuf[cur] = _dot(a, wd_ref[0])
    for cp in copies:
        cp.wait()

    @pl.when(i == last)
    def _():
        run(scatter(dst_cur, cur))


def _experts(h2, tile_expert, src, dst, n_out, wg, wu, wd, tm):
    d, f = wg.shape[1], wg.shape[2]
    n_tiles = src.shape[0] // tm
    src3, dst3 = src.reshape(n_tiles, 1, tm), dst.reshape(n_tiles, 1, tm)

    def idx_spec(fn):
        return pl.BlockSpec((1, 1, tm), lambda i, te: (fn(i), 0, 0), memory_space=pltpu.SMEM)

    def w_spec(shape):
        return pl.BlockSpec((1,) + shape, lambda i, te: (te[i], 0, 0), pipeline_mode=pl.Buffered(1))

    grid_spec = pltpu.PrefetchScalarGridSpec(
        num_scalar_prefetch=1,
        grid=(n_tiles,),
        in_specs=[
            idx_spec(lambda i: i),
            idx_spec(lambda i: jnp.minimum(i + 1, n_tiles - 1)),
            idx_spec(lambda i: jnp.maximum(i - 1, 0)),
            idx_spec(lambda i: i),
            pl.BlockSpec(memory_space=pl.ANY),
            w_spec((d, f)), w_spec((d, f)), w_spec((f, d)),
        ],
        out_specs=pl.BlockSpec(memory_space=pl.ANY),
        scratch_shapes=[pltpu.VMEM((2, tm, d), F32), pltpu.VMEM((2, tm, d), F32),
                        pltpu.SemaphoreType.DMA(()), pltpu.SemaphoreType.DMA(())],
    )
    return pl.pallas_call(
        functools.partial(_expert_kernel, tm=tm),
        out_shape=jax.ShapeDtypeStruct((n_out, d), F32),
        grid_spec=grid_spec,
        compiler_params=_params(("arbitrary",)),
        name="moe_experts",
    )(tile_expert, src3, src3, dst3, dst3, h2, wg.astype(BF16), wu.astype(BF16), wd.astype(BF16))


def _combine_kernel(x_ref, mod_ref, g_ref, route_ref, y0_ref, y1_ref, o_ref):
    x = x_ref[0]
    gt = mod_ref[0, 5:6, :]
    route = route_ref[0]
    y = route[:, 2:3] * y0_ref[...] + route[:, 3:4] * y1_ref[...]
    o_ref[0] = x + gt * (_rms(y) * g_ref[3:4, :])


def _combine(x, mod, norm_g, route, ys, tm):
    b, s, d = x.shape
    nj = s // tm
    tok = pl.BlockSpec((1, tm, d), lambda i, j: (i, j, 0))
    return pl.pallas_call(
        _combine_kernel,
        out_shape=jax.ShapeDtypeStruct((b, s, d), F32),
        grid=(b, nj),
        in_specs=[
            tok,
            pl.BlockSpec((1,) + mod.shape[1:], lambda i, j: (i, 0, 0)),
            _const_spec(norm_g.shape),
            pl.BlockSpec((1, tm, LANE), lambda i, j: (i, j, 0)),
            pl.BlockSpec((tm, d), lambda i, j: (i * nj + j, 0)),
            pl.BlockSpec((tm, d), lambda i, j: (b * nj + i * nj + j, 0)),
        ],
        out_specs=tok,
        compiler_params=_params(("arbitrary", "arbitrary")),
        name="moe_combine",
    )(x, mod, norm_g, route, ys, ys)


def _pair_of_row_kernel(pos_ref, pair_ref, *, chunk, n_rows):
    i = pl.program_id(0)

    @pl.when(i == 0)
    def _():
        def fill(k, carry):
            pair_ref[k] = -1
            return carry

        lax.fori_loop(0, n_rows, fill, 0, unroll=PAIR_UNROLL)

    def put(k, carry):
        pair_ref[pos_ref[0, 0, k]] = i * chunk + k
        return carry

    lax.fori_loop(0, chunk, put, 0, unroll=PAIR_UNROLL)


def _pair_of_row(pos, n_rows):
    n = pos.shape[0]
    chunk = min(PAIR_CHUNK, n)
    return pl.pallas_call(
        functools.partial(_pair_of_row_kernel, chunk=chunk, n_rows=n_rows),
        out_shape=jax.ShapeDtypeStruct((n_rows,), jnp.int32),
        grid=(n // chunk,),
        in_specs=[pl.BlockSpec((1, 1, chunk), lambda i: (i, 0, 0), memory_space=pltpu.SMEM)],
        out_specs=pl.BlockSpec(memory_space=pltpu.SMEM),
        compiler_params=_params(("arbitrary",)),
        name="moe_pair_of_row",
    )(pos.reshape(n // chunk, 1, chunk))


def _route_maps(idx, n_exp, tm):
    flat = idx.reshape(-1)
    n_pairs = flat.shape[0]
    n_tok = n_pairs // TOP_K
    onehot = (flat[:, None] == jnp.arange(n_exp, dtype=jnp.int32)[None, :]).astype(jnp.int32)
    csum = jnp.cumsum(onehot, axis=0)
    rank = jnp.sum(onehot * csum, axis=1) - 1
    counts = csum[-1]
    tiles = (counts + tm - 1) // tm
    tile_end = jnp.cumsum(tiles)
    tile_start = tile_end - tiles
    pos = tile_start[flat] * tm + rank
    n_tiles = n_pairs // tm + n_exp
    tile_ids = jnp.arange(n_tiles, dtype=jnp.int32)
    tile_expert = jnp.minimum(jnp.sum((tile_end[None, :] <= tile_ids[:, None]).astype(jnp.int32), axis=1),
                              n_exp - 1).astype(jnp.int32)
    n_rows = n_tiles * tm
    pair = _pair_of_row(pos.astype(jnp.int32), n_rows)
    valid = pair >= 0
    tok, slot = pair // TOP_K, pair % TOP_K
    src = jnp.where(valid, tok, 0).astype(jnp.int32)
    spare = n_pairs + jnp.arange(n_rows, dtype=jnp.int32) % tm
    dst = jnp.where(valid, slot * n_tok + tok, spare).astype(jnp.int32)
    return tile_expert, src, dst, n_pairs + tm


def _moe(x, h2, route, mod, norm_g, wg, wu, wd, tm, tm_e):
    b, s, d = x.shape
    idx = route[..., 0:TOP_K].astype(jnp.int32).reshape(b * s, TOP_K)
    tile_expert, src, dst, n_out = _route_maps(idx, wg.shape[0], tm_e)
    ys = _experts(h2.reshape(b * s, d), tile_expert, src, dst, n_out, wg, wu, wd, tm_e)
    return _combine(x, mod, norm_g, route, ys, tm)


def _tile(s, want):
    t = min(want, s)
    assert s % t == 0 and t % CHUNK == 0
    return t


def kernel(x, c, ada_w, ada_b, norm_g, mix_w_in, conv_w, pool_w, pool_scale, mix_w_out, ffn_w_gate, ffn_w_up, ffn_w_down, rwkv_mu, rwkv_w_r, rwkv_w_k, rwkv_w_v, rwkv_w_o, rwkv_w0, rwkv_w1, rwkv_w2, rwkv_a0, rwkv_a1, rwkv_a2, rwkv_g1, rwkv_g2, rwkv_k_k, rwkv_k_a, rwkv_r_k, rwkv_ln_g, rwkv_ln_b, moe_router, moe_w_gate, moe_w_up, moe_w_down):
    depth = ada_w.shape[0]
    s = x.shape[1]
    tm = _tile(s, 512)
    mods = _ada(c, ada_w, ada_b)
    for layer in range(depth):
        mod, ng, i = mods[layer], norm_g[layer], layer // 2
        if layer % 2 == 0:
            x = _mixer(x, mod, ng, mix_w_in[i], conv_w[i], pool_w[i], pool_scale[i], mix_w_out[i], tm)
            x = _ffn(x, mod, ng, ffn_w_gate[i], ffn_w_up[i], ffn_w_down[i], tm)
        else:
            r, k, v, kk, a, ld, gate = _rwkv_proj(
                x, mod, ng, rwkv_mu[i], rwkv_w_r[i], rwkv_w_k[i], rwkv_w_v[i], rwkv_w0[i], rwkv_w1[i], rwkv_w2[i],
                rwkv_a0[i], rwkv_a1[i], rwkv_a2[i], rwkv_g1[i], rwkv_g2[i], rwkv_k_k[i], rwkv_k_a[i], tm)
            y, bonus = _rwkv_scan(r, k, v, kk, a, ld, rwkv_r_k[i])
            x = _rwkv_out(x, mod, ng, y, bonus, gate, rwkv_ln_g[i], rwkv_ln_b[i], rwkv_w_o[i], tm)
            h2, route = _router(x, mod, ng, moe_router[i], tm)
            x = _moe(x, h2, route, mod, ng, moe_w_gate[i], moe_w_up[i], moe_w_down[i], tm, tm)
    return x
```
